```python
import math
import jax, jax.numpy as jnp
from jax import lax
import numpy as np

D_MODEL = 1024
BATCH = 4
SEQ = 4096
DEPTH = 2
DEC_BATCH = 8
DEC_SEQ = 64
PAST_LEN = 1024

CHUNK = 64
Q_BLOCK = 128
H_G = 4
DV_G = D_MODEL // (2 * H_G)
DK_G = DV_G // 2
GATE_RANK = 16
GATE_NORMALIZER = 16.0
H_D = 4
DH_D = D_MODEL // (4 * H_D)
NUM_BUCKETS = 32
MAX_DISTANCE = 128
D_FF = 2816
CONV_W = 3
EPS = 1e-6

PROJ_SPLITS = (H_G * DK_G, H_G * DK_G, H_G * DV_G, H_G * DV_G, GATE_RANK,
               H_D * 2 * DH_D, H_D * 2 * DH_D, H_D * 2 * DH_D)
D_PROJ = sum(PROJ_SPLITS)
MIX_WIDTH = H_G * DV_G + H_D * 2 * DH_D

kernel_name = "hybrid_gla_diffattn_streaming_step"


def rms_norm(x, w):
    xf = x.astype(jnp.float32)
    y = xf * lax.rsqrt(jnp.mean(xf * xf, axis=-1, keepdims=True) + EPS)
    return (y * w.astype(jnp.float32)).astype(x.dtype)


def lambda_init(layer):
    return 0.8 - 0.6 * math.exp(-0.3 * layer)


def t5_bias(qpos, kpos, table):
    rel = kpos[None, :] - qpos[:, None]
    nb = NUM_BUCKETS // 2
    max_exact = nb // 2
    ret = (rel > 0).astype(jnp.int32) * nb
    n = jnp.abs(rel)
    nf = jnp.maximum(n, 1).astype(jnp.float32)
    large = max_exact + (jnp.log(nf / max_exact) / math.log(MAX_DISTANCE / max_exact)
                         * (nb - max_exact)).astype(jnp.int32)
    large = jnp.minimum(large, nb - 1)
    bucket = ret + jnp.where(n < max_exact, n, large)
    return jnp.transpose(table[bucket].astype(jnp.float32), (2, 0, 1))


def gla_chunked(q, k, v, log_a, s0):
    B, T, H, _ = q.shape
    DV = v.shape[-1]
    C = min(CHUNK, T)
    N = T // C

    def blocks(t):
        return t.reshape(B, N, C, H, t.shape[-1]).transpose(1, 0, 3, 2, 4)

    q, k, v, log_a = blocks(q), blocks(k), blocks(v), blocks(log_a)
    G = jnp.cumsum(log_a, axis=-2)
    G_last = G[..., -1:, :]
    q_dec = q * jnp.exp(G)
    k_inv = k * jnp.exp(-G)
    k_end = k * jnp.exp(G_last - G)
    causal = jnp.tril(jnp.ones((C, C), dtype=bool))
    A = jnp.where(causal, jnp.einsum('nbhid,nbhjd->nbhij', q_dec, k_inv), 0.0)
    o_intra = jnp.einsum('nbhij,nbhjv->nbhiv', A, v)
    upd = jnp.einsum('nbhjd,nbhjv->nbhdv', k_end, v)
    decay = jnp.exp(G_last[..., 0, :])

    def step(S, inp):
        dec, u, qd = inp
        o = jnp.einsum('bhid,bhdv->bhiv', qd, S)
        return dec[..., None] * S + u, o

    s_final, o_inter = lax.scan(step, s0, (decay, upd, q_dec))
    o = (o_intra + o_inter).transpose(1, 0, 3, 2, 4).reshape(B, T, H, DV)
    return o, s_final


def diff_attention(q, k, v, pos0, lam, table):
    B, H, T, _, DH = q.shape
    K = k.shape[2]
    QB = min(Q_BLOCK, T)
    nblk = T // QB
    scale = DH ** -0.5
    kpos = jnp.arange(K, dtype=jnp.int32)
    qb_all = q.reshape(B, H, nblk, QB, 2, DH).transpose(2, 0, 1, 3, 4, 5)
    starts = pos0 + jnp.arange(nblk, dtype=jnp.int32) * QB

    def one(args):
        qb, start = args
        qpos = start + jnp.arange(QB, dtype=jnp.int32)
        mask = (kpos // CHUNK)[None, :] <= (qpos // CHUNK)[:, None]
        bias = t5_bias(qpos, kpos, table)
        s = jnp.einsum('bhqmd,bhkmd->mbhqk', qb, k) * scale + bias
        s = jnp.where(mask, s, -jnp.inf)
        p = jax.nn.softmax(s, axis=-1)
        a = p[0] - lam * p[1]
        return jnp.einsum('bhqk,bhkv->bhqv', a, v)

    out = lax.map(one, (qb_all, starts))
    return out.transpose(1, 2, 0, 3, 4).reshape(B, H, T, 2 * DH)


def trunk_layer(x, layer, past_k, past_v, gla_state, conv_state, t5_table,
                w_in, w_gate_up, b_gate_up, gla_norm, lam_params, diff_norm, w_out,
                ln_mix_pre, ln_mix_post, ln_ffn_pre, ln_ffn_post,
                w_ffn_up, conv_w, conv_b, w_ffn_down):
    f32 = jnp.float32
    B, T, _ = x.shape
    pos0 = past_k.shape[2]
    h = rms_norm(x, ln_mix_pre)
    proj = h @ w_in
    idx = np.cumsum(PROJ_SPLITS)[:-1].tolist()
    qg, kg, vg, rg, gd, qd, kd, vd = jnp.split(proj, idx, axis=-1)
    qg = qg.reshape(B, T, H_G, DK_G).astype(f32) * DK_G ** -0.5
    kg = kg.reshape(B, T, H_G, DK_G).astype(f32)
    vg = vg.reshape(B, T, H_G, DV_G).astype(f32)
    log_a = jax.nn.log_sigmoid((gd @ w_gate_up + b_gate_up).astype(f32)) / GATE_NORMALIZER
    log_a = log_a.reshape(B, T, H_G, DK_G)
    og, gla_new = gla_chunked(qg, kg, vg, log_a, gla_state.astype(f32))
    og = rms_norm(og, gla_norm).reshape(B, T, H_G * DV_G) * jax.nn.silu(rg.astype(f32))
    qd = qd.reshape(B, T, H_D, 2, DH_D).transpose(0, 2, 1, 3, 4).astype(f32)
    kd = kd.reshape(B, T, H_D, 2 * DH_D).transpose(0, 2, 1, 3)
    vd = vd.reshape(B, T, H_D, 2 * DH_D).transpose(0, 2, 1, 3)
    k_all = jnp.concatenate([past_k.astype(kd.dtype), kd], axis=2).astype(f32)
    k_all = k_all.reshape(B, H_D, pos0 + T, 2, DH_D)
    v_all = jnp.concatenate([past_v.astype(vd.dtype), vd], axis=2).astype(f32)
    lp = lam_params.astype(f32)
    lam_0 = lambda_init(layer)
    lam = jnp.exp(jnp.sum(lp[0] * lp[1])) - jnp.exp(jnp.sum(lp[2] * lp[3])) + lam_0
    od = diff_attention(qd, k_all, v_all, pos0, lam, t5_table)
    od = rms_norm(od, diff_norm) * (1.0 - lam_0)
    od = od.transpose(0, 2, 1, 3).reshape(B, T, H_D * 2 * DH_D)
    mix = jnp.concatenate([og, od], axis=-1).astype(x.dtype) @ w_out
    x = x + rms_norm(mix, ln_mix_post)
    h = rms_norm(x, ln_ffn_pre)
    up = h @ w_ffn_up
    ext = jnp.concatenate([conv_state.astype(up.dtype), up], axis=1)
    c = conv_b
    for i in range(CONV_W):
        c = c + conv_w[i] * ext[:, i:i + T]
    g, u = jnp.split(c, 2, axis=-1)
    y = jax.nn.gelu(g, approximate=True) * u
    x = x + rms_norm(y @ w_ffn_down, ln_ffn_post)
    return (x, kd, vd, gla_new.astype(gla_state.dtype), ext[:, -(CONV_W - 1):])


def setup_inputs(seed: int = 0) -> dict:
    key = jax.random.key(seed)
    ks = jax.random.split(key, 24)
    nrm = lambda k, shape, s=1.0: jax.random.normal(k, shape, jnp.float32) * s
    F2 = 2 * D_FF
    return {
        "x_prompt": nrm(ks[0], (BATCH, SEQ, D_MODEL)),
        "x_sample": nrm(ks[1], (DEC_BATCH, DEC_SEQ, D_MODEL)),
        "cache_k": nrm(ks[2], (DEPTH, DEC_BATCH, H_D, PAST_LEN, 2 * DH_D)),
        "cache_v": nrm(ks[3], (DEPTH, DEC_BATCH, H_D, PAST_LEN, 2 * DH_D)),
        "state_gla": nrm(ks[4], (DEPTH, DEC_BATCH, H_G, DK_G, DV_G)),
        "state_conv": nrm(ks[5], (DEPTH, DEC_BATCH, CONV_W - 1, F2)),
        "t5_table": nrm(ks[6], (NUM_BUCKETS, H_D), 0.5),
        "w_in": nrm(ks[7], (DEPTH, D_MODEL, D_PROJ), D_MODEL ** -0.5),
        "w_gate_up": nrm(ks[8], (DEPTH, GATE_RANK, H_G * DK_G), GATE_RANK ** -0.5),
        "b_gate_up": nrm(ks[9], (DEPTH, H_G * DK_G), 0.1),
        "gla_norm": 1.0 + nrm(ks[10], (DEPTH, DV_G), 0.02),
        "lam_params": nrm(ks[11], (DEPTH, 4, DH_D), 0.1),
        "diff_norm": 1.0 + nrm(ks[12], (DEPTH, 2 * DH_D), 0.02),
        "w_out": nrm(ks[13], (DEPTH, MIX_WIDTH, D_MODEL), MIX_WIDTH ** -0.5),
        "ln_mix_pre": 1.0 + nrm(ks[14], (DEPTH, D_MODEL), 0.02),
        "ln_mix_post": 1.0 + nrm(ks[15], (DEPTH, D_MODEL), 0.02),
        "ln_ffn_pre": 1.0 + nrm(ks[16], (DEPTH, D_MODEL), 0.02),
        "ln_ffn_post": 1.0 + nrm(ks[17], (DEPTH, D_MODEL), 0.02),
        "w_ffn_up": nrm(ks[18], (DEPTH, D_MODEL, F2), D_MODEL ** -0.5),
        "conv_w": nrm(ks[19], (DEPTH, CONV_W, F2), CONV_W ** -0.5),
        "conv_b": nrm(ks[20], (DEPTH, F2), 0.01),
        "w_ffn_down": nrm(ks[21], (DEPTH, D_FF, D_MODEL), D_FF ** -0.5),
    }


def reference(x_prompt, x_sample, cache_k, cache_v, state_gla, state_conv, t5_table,
              w_in, w_gate_up, b_gate_up, gla_norm, lam_params, diff_norm, w_out,
              ln_mix_pre, ln_mix_post, ln_ffn_pre, ln_ffn_post,
              w_ffn_up, conv_w, conv_b, w_ffn_down):
    dt = x_prompt.dtype

    def layer_weights(l):
        return (t5_table, w_in[l], w_gate_up[l], b_gate_up[l], gla_norm[l], lam_params[l],
                diff_norm[l], w_out[l], ln_mix_pre[l], ln_mix_post[l], ln_ffn_pre[l],
                ln_ffn_post[l], w_ffn_up[l], conv_w[l], conv_b[l], w_ffn_down[l])

    y = x_prompt
    kp, vp, gp, cp = [], [], [], []
    for l in range(DEPTH):
        empty_k = jnp.zeros((BATCH, H_D, 0, 2 * DH_D), dt)
        zero_s = jnp.zeros((BATCH, H_G, DK_G, DV_G), jnp.float32)
        zero_c = jnp.zeros((BATCH, CONV_W - 1, 2 * D_FF), dt)
        y, k_new, v_new, s_new, c_new = trunk_layer(y, l, empty_k, empty_k, zero_s, zero_c,
                                                   *layer_weights(l))
        kp.append(k_new); vp.append(v_new); gp.append(s_new); cp.append(c_new)
    y_prompt = y

    y = x_sample
    ksm, vsm, gsm, csm = [], [], [], []
    for l in range(DEPTH):
        y, k_new, v_new, s_new, c_new = trunk_layer(y, l, cache_k[l], cache_v[l], state_gla[l],
                                                   state_conv[l], *layer_weights(l))
        ksm.append(k_new); vsm.append(v_new); gsm.append(s_new); csm.append(c_new)
    y_sample = y

    return (y_prompt, y_sample,
            jnp.stack(kp), jnp.stack(vp), jnp.stack(gp), jnp.stack(cp),
            jnp.stack(ksm), jnp.stack(vsm), jnp.stack(gsm), jnp.stack(csm))
```

```python
import functools
import math

import jax
import jax.numpy as jnp
from jax import lax
from jax.experimental import pallas as pl
from jax.experimental.pallas import tpu as pltpu

F32 = jnp.float32
BF16 = jnp.bfloat16

CHUNK = 64
H_G = 4
DK_G = 64
DV_G = 128
GATE_RANK = 16
GATE_NORMALIZER = 16.0
H_D = 4
DH_D = 64
NUM_BUCKETS = 32
MAX_DISTANCE = 128
CONV_W = 3
EPS = 1e-6

GQ = H_G * DK_G
GV = H_G * DV_G
DA = H_D * 2 * DH_D
HD2 = 2 * DH_D

VMEM_LIMIT_BYTES = 56 * 1024 * 1024
ROW_TILE = 512
GLA_ROWS = 256
ATTN_TILE = 256
NEG_INIT = -1e30


def _rms(x, w):
    return x * lax.rsqrt(jnp.mean(x * x, axis=-1, keepdims=True) + EPS) * w


def _dot(a, b):
    return jnp.dot(a, b, preferred_element_type=F32)


def _dot_nt(a, b):
    return lax.dot_general(a, b, (((1,), (1,)), ((), ())), preferred_element_type=F32)


def _params(n_grid):
    return pltpu.CompilerParams(dimension_semantics=("arbitrary",) * n_grid,
                                vmem_limit_bytes=VMEM_LIMIT_BYTES)


def _const_spec(shape):
    nd = len(shape)
    return pl.BlockSpec(shape, lambda *_: (0,) * nd, pipeline_mode=pl.Buffered(1))


def _bias_kernel(tab_ref, o_ref, *, qstarts, nq, nk):
    h = pl.program_id(0)
    nb = NUM_BUCKETS // 2
    max_exact = nb // 2
    for d, q0 in enumerate(qstarts):
        qpos = q0 + lax.broadcasted_iota(jnp.int32, (nq, nk), 0)
        kpos = lax.broadcasted_iota(jnp.int32, (nq, nk), 1)
        rel = kpos - qpos
        ret = jnp.where(rel > 0, nb, 0)
        n = jnp.abs(rel)
        nf = jnp.maximum(n, 1).astype(F32)
        large = max_exact + (jnp.log(nf / max_exact) / math.log(MAX_DISTANCE / max_exact)
                             * (nb - max_exact)).astype(jnp.int32)
        large = jnp.minimum(large, nb - 1)
        bucket = ret + jnp.where(n < max_exact, n, large)
        bias = jnp.zeros((nq, nk), F32)
        for b in range(NUM_BUCKETS):
            bias = jnp.where(bucket == b, tab_ref[b, h], bias)
        visible = (kpos // CHUNK) <= (qpos // CHUNK)
        o_ref[0, d] = jnp.where(visible, bias, -jnp.inf)


def _bias_tiles(t5_table, qstarts, nq, nk):
    nd = len(qstarts)
    return pl.pallas_call(
        functools.partial(_bias_kernel, qstarts=tuple(qstarts), nq=nq, nk=nk),
        grid=(H_D,),
        in_specs=[pl.BlockSpec(memory_space=pltpu.SMEM)],
        out_specs=pl.BlockSpec((1, nd, nq, nk), lambda h: (h, 0, 0, 0)),
        out_shape=jax.ShapeDtypeStruct((H_D, nd, nq, nk), F32),
        compiler_params=_params(1),
        name="t5_bias_tiles",
    )(t5_table)


def _proj_kernel(x_ref, ln_ref, w_ref, wgd_ref, wgu_ref, bgu_ref,
                 gq_ref, gk_ref, gv_ref, gr_ref, la_ref, dq_ref, dk_ref, dv_ref, *bf_refs,
                 n_seq, seq_rows):
    h = _rms(x_ref[...], ln_ref[...]).astype(BF16)

    def seg(a, b):
        return _dot(h, w_ref[:, a:b])

    gq_ref[...] = seg(0, GQ) * DK_G ** -0.5
    gk_ref[...] = seg(GQ, 2 * GQ)
    gv_ref[...] = seg(2 * GQ, 2 * GQ + GV)
    gr_ref[...] = seg(2 * GQ + GV, 2 * GQ + 2 * GV)
    gd = _dot(h, wgd_ref[...]).astype(BF16)
    z = _dot(gd, wgu_ref[...]) + bgu_ref[...]
    log_sig = jnp.minimum(z, 0.0) - jnp.log1p(jnp.exp(-jnp.abs(z)))
    la_ref[...] = log_sig / GATE_NORMALIZER

    base = 2 * GQ + 2 * GV
    q = seg(base, base + DA) * DH_D ** -0.5
    k = seg(base + DA, base + 2 * DA)
    v = seg(base + 2 * DA, base + 3 * DA)
    for s in range(n_seq):
        rows = slice(s * seq_rows, (s + 1) * seq_rows)
        for hh in range(H_D):
            cols = slice(hh * HD2, (hh + 1) * HD2)
            dq_ref[s, hh] = q[rows, cols]
            dk_ref[s, hh] = k[rows, cols]
            dv_ref[s, hh] = v[rows, cols]
            if bf_refs:
                bf_refs[0][s, hh] = k[rows, cols].astype(BF16)
                bf_refs[1][s, hh] = v[rows, cols].astype(BF16)


def _proj(x2d, B, T, ln, w_main, w_gd, w_gu, b_gu, *, emit_bf16):
    n_rows, D = x2d.shape
    if T >= ROW_TILE:
        tm, n_seq, seq_rows, tps = ROW_TILE, 1, ROW_TILE, T // ROW_TILE
    else:
        tm, n_seq, seq_rows, tps = n_rows, B, T, 1
    grid = (n_rows // tm,)
    row = lambda w: pl.BlockSpec((tm, w), lambda i: (i, 0))
    head = pl.BlockSpec((n_seq, H_D, seq_rows, HD2), lambda i: (i // tps, 0, i % tps, 0))
    f32_rows = lambda w: jax.ShapeDtypeStruct((n_rows, w), F32)
    head_shape = lambda dt: jax.ShapeDtypeStruct((B, H_D, T, HD2), dt)
    out_specs = [row(GQ), row(GQ), row(GV), row(GV), row(GQ), head, head, head]
    out_shape = [f32_rows(GQ), f32_rows(GQ), f32_rows(GV), f32_rows(GV), f32_rows(GQ),
                 head_shape(F32), head_shape(F32), head_shape(F32)]
    if emit_bf16:
        out_specs += [head, head]
        out_shape += [head_shape(BF16), head_shape(BF16)]
    return pl.pallas_call(
        functools.partial(_proj_kernel, n_seq=n_seq, seq_rows=seq_rows),
        grid=grid,
        in_specs=[row(D), _const_spec((1, D)), _const_spec(w_main.shape), _const_spec(w_gd.shape),
                  _const_spec(w_gu.shape), _const_spec((1, GQ))],
        out_specs=out_specs,
        out_shape=out_shape,
        compiler_params=_params(1),
        name="proj",
    )(x2d, ln, w_main, w_gd, w_gu, b_gu)


def _gla_kernel(q_ref, k_ref, v_ref, r_ref, la_ref, s0_ref, gn_ref, o_ref, sout_ref, s_ref, *, n_chunks):
    t = pl.program_id(1)
    srow = lax.broadcasted_iota(jnp.int32, (GQ, GV), 0) // DK_G
    scol = lax.broadcasted_iota(jnp.int32, (GQ, GV), 1) // DV_G
    on_diag = srow == scol

    @pl.when(t == 0)
    def _():
        s_ref[...] = jnp.zeros((GQ, GV), F32)
        for hh in range(H_G):
            s_ref[hh * DK_G:(hh + 1) * DK_G, hh * DV_G:(hh + 1) * DV_G] = s0_ref[0, hh]

    ri = lax.broadcasted_iota(jnp.int32, (CHUNK, CHUNK), 0)
    ci = lax.broadcasted_iota(jnp.int32, (CHUNK, CHUNK), 1)
    causal = ci <= ri
    tril = causal.astype(BF16)
    head_of_lane = lax.broadcasted_iota(jnp.int32, (CHUNK, GQ), 1) // DK_G
    gn = gn_ref[...]

    for c in range(n_chunks):
        rows = slice(c * CHUNK, (c + 1) * CHUNK)
        la = la_ref[0, rows, :]
        a1 = la.astype(BF16)
        r1 = la - a1.astype(F32)
        a2 = r1.astype(BF16)
        a3 = (r1 - a2.astype(F32)).astype(BF16)
        g = _dot(tril, a1) + _dot(tril, a2) + _dot(tril, a3)
        g_last = g[CHUNK - 1:CHUNK, :]
        q = q_ref[0, rows, :]
        k = k_ref[0, rows, :]
        v = v_ref[0, rows, :].astype(BF16)
        q_dec = q * jnp.exp(g)
        k_inv = (k * jnp.exp(-g)).astype(BF16)
        k_end = k * jnp.exp(g_last - g)
        decay_col = jnp.exp(jnp.sum(la.T, axis=1, keepdims=True))

        s_old = s_ref[...]
        o_inter = _dot(q_dec.astype(BF16), s_old.astype(BF16))
        upd = _dot(k_end.T.astype(BF16), v)
        s_ref[...] = decay_col * s_old + jnp.where(on_diag, upd, 0.0)

        for hh in range(H_G):
            vcols = slice(hh * DV_G, (hh + 1) * DV_G)
            qm = jnp.where(head_of_lane == hh, q_dec, 0.0).astype(BF16)
            a = jnp.where(causal, _dot_nt(qm, k_inv), 0.0)
            o = _dot(a.astype(BF16), v[:, vcols]) + o_inter[:, vcols]
            r = r_ref[0, rows, vcols]
            o_ref[0, rows, vcols] = _rms(o, gn) * (r * (1.0 / (1.0 + jnp.exp(-r))))

    for hh in range(H_G):
        sout_ref[0, hh] = s_ref[hh * DK_G:(hh + 1) * DK_G, hh * DV_G:(hh + 1) * DV_G]


def _gla(gq, gk, gv, gr, la, s0, gn, B, T):
    tb = min(GLA_ROWS, T)
    r3 = lambda a: a.reshape(B, T, a.shape[-1])
    blk = lambda w: pl.BlockSpec((1, tb, w), lambda b, t: (b, t, 0))
    st = pl.BlockSpec((1, H_G, DK_G, DV_G), lambda b, t: (b, 0, 0, 0))
    return pl.pallas_call(
        functools.partial(_gla_kernel, n_chunks=tb // CHUNK),
        grid=(B, T // tb),
        in_specs=[blk(GQ), blk(GQ), blk(GV), blk(GV), blk(GQ), st, _const_spec((1, DV_G))],
        out_specs=[blk(GV), st],
        out_shape=[jax.ShapeDtypeStruct((B, T, GV), F32),
                   jax.ShapeDtypeStruct((B, H_G, DK_G, DV_G), F32)],
        scratch_shapes=[pltpu.VMEM((GQ, GV), F32)],
        compiler_params=_params(2),
        name="gla",
    )(r3(gq), r3(gk), r3(gv), r3(gr), r3(la), s0, gn)


def _lam(lp_ref, lam0):
    lp = lp_ref[...]
    t1 = jnp.sum(lp[0:1] * lp[1:2], axis=-1, keepdims=True)
    t2 = jnp.sum(lp[2:3] * lp[3:4], axis=-1, keepdims=True)
    return jnp.exp(t1) - jnp.exp(t2) + lam0


def _split_maps(q):
    lane = lax.broadcasted_iota(jnp.int32, q.shape, 1)
    return (jnp.where(lane < DH_D, q, 0.0).astype(BF16),
            jnp.where(lane >= DH_D, q, 0.0).astype(BF16))


def _attn_finish(acc0, l0, acc1, l1, lam, dn, lam0):
    o = acc0 * (1.0 / l0) - lam * (acc1 * (1.0 / l1))
    return _rms(o, dn) * (1.0 - lam0)


def _attn_prompt_kernel(lp_ref, q_ref, k_ref, v_ref, bias_ref, dn_ref, o_ref, *, tile, lam0):
    i = pl.program_id(2)
    qm = _split_maps(q_ref[0, 0])

    def body(jj, carry):
        j = i - jj
        ks = pl.multiple_of(j * tile, tile)
        kt = k_ref[0, 0, pl.ds(ks, tile), :]
        vt = v_ref[0, 0, pl.ds(ks, tile), :]
        bias = bias_ref[0, jnp.minimum(jj, 2)]
        out = []
        for m in range(2):
            m_old, l_old, acc = carry[3 * m:3 * m + 3]
            s = _dot_nt(qm[m], kt) + bias
            m_new = jnp.maximum(m_old, jnp.max(s, axis=-1, keepdims=True))
            p = jnp.exp(s - m_new)
            alpha = jnp.exp(m_old - m_new)
            l_new = alpha * l_old + jnp.sum(p, axis=-1, keepdims=True)
            acc = alpha * acc + _dot(p.astype(BF16), vt)
            out += [m_new, l_new, acc]
        return tuple(out)

    init = (jnp.full((tile, 1), NEG_INIT, F32), jnp.zeros((tile, 1), F32), jnp.zeros((tile, HD2), F32)) * 2
    m0, l0, acc0, m1, l1, acc1 = lax.fori_loop(0, i + 1, body, init)
    o_ref[0] = _attn_finish(acc0, l0, acc1, l1, _lam(lp_ref, lam0), dn_ref[...], lam0)


def _attn_prompt(lp, dq, kb, vb, bias, dn, lam0):
    B, H, T, _ = dq.shape
    tile = bias.shape[-1]
    kv = pl.BlockSpec((1, 1, T, HD2), lambda b, h, i: (b, h, 0, 0))
    return pl.pallas_call(
        functools.partial(_attn_prompt_kernel, tile=tile, lam0=lam0),
        grid=(B, H, T // tile),
        in_specs=[_const_spec(lp.shape),
                  pl.BlockSpec((1, 1, tile, HD2), lambda b, h, i: (b, h, i, 0)),
                  kv, kv,
                  pl.BlockSpec((1, 3, tile, tile), lambda b, h, i: (h, 0, 0, 0)),
                  _const_spec((1, HD2))],
        out_specs=pl.BlockSpec((1, tile, HD2), lambda b, h, i: (b, i, h)),
        out_shape=jax.ShapeDtypeStruct((B, T, H * HD2), F32),
        compiler_params=_params(3),
        name="attn_prompt",
    )(lp, dq, kb, vb, bias, dn)


def _attn_step_kernel(lp_ref, q_ref, kp_ref, vp_ref, kn_ref, vn_ref, bias_ref, dn_ref, o_ref, *, past, lam0):
    qm = _split_maps(q_ref[0, 0])
    kp = kp_ref[0, 0].astype(BF16)
    vp = vp_ref[0, 0].astype(BF16)
    kn = kn_ref[0, 0].astype(BF16)
    vn = vn_ref[0, 0].astype(BF16)
    bias_p = bias_ref[0, 0, :, :past]
    bias_n = bias_ref[0, 0, :, past:]
    res = []
    for m in range(2):
        s_p = _dot_nt(qm[m], kp) + bias_p
        s_n = _dot_nt(qm[m], kn) + bias_n
        mx = jnp.maximum(jnp.max(s_p, axis=-1, keepdims=True), jnp.max(s_n, axis=-1, keepdims=True))
        p_p = jnp.exp(s_p - mx)
        p_n = jnp.exp(s_n - mx)
        l = jnp.sum(p_p, axis=-1, keepdims=True) + jnp.sum(p_n, axis=-1, keepdims=True)
        acc = _dot(p_p.astype(BF16), vp) + _dot(p_n.astype(BF16), vn)
        res += [acc, l]
    o_ref[0] = _attn_finish(res[0], res[1], res[2], res[3], _lam(lp_ref, lam0), dn_ref[...], lam0)


def _attn_step(lp, dq, k_past, v_past, dk, dv, bias, dn, lam0):
    B, H, T, _ = dq.shape
    past = k_past.shape[2]
    new = pl.BlockSpec((1, 1, T, HD2), lambda b, h: (b, h, 0, 0))
    old = pl.BlockSpec((1, 1, past, HD2), lambda b, h: (b, h, 0, 0))
    return pl.pallas_call(
        functools.partial(_attn_step_kernel, past=past, lam0=lam0),
        grid=(B, H),
        in_specs=[_const_spec(lp.shape), new, old, old, new, new,
                  pl.BlockSpec((1, 1, T, past + T), lambda b, h: (h, 0, 0, 0)),
                  _const_spec((1, HD2))],
        out_specs=pl.BlockSpec((1, T, HD2), lambda b, h: (b, 0, h)),
        out_shape=jax.ShapeDtypeStruct((B, T, H * HD2), F32),
        compiler_params=_params(2),
        name="attn_step",
    )(lp, dq, k_past, v_past, dk, dv, bias, dn)


def _shift_rows(up, prev, n_seq, seq_rows):
    row = lax.broadcasted_iota(jnp.int32, up.shape, 0)
    d1 = pltpu.roll(up, 1, 0)
    d2 = pltpu.roll(up, 2, 0)
    for s in range(n_seq):
        first = s * seq_rows
        d1 = jnp.where(row == first, prev[s][1:2], d1)
        d2 = jnp.where(row == first, prev[s][0:1], d2)
        d2 = jnp.where(row == first + 1, prev[s][1:2], d2)
    return d1, d2


def _ffn_kernel(x_ref, og_ref, od_ref, wo_ref, lpost_ref, lpre_ref, lout_ref, wup_ref, cw_ref, cb_ref,
                cs_ref, wdn_ref, y_ref, cnew_ref, prev_ref, *, n_seq, seq_rows, tiles_per_seq, d_ff, fc):
    ti = pl.program_id(0) % tiles_per_seq

    @pl.when(ti == 0)
    def _():
        prev_ref[...] = cs_ref[...]

    mix = _dot(og_ref[...].astype(BF16), wo_ref[:GV, :]) + _dot(od_ref[...].astype(BF16), wo_ref[GV:, :])
    x1 = x_ref[...] + _rms(mix, lpost_ref[...])
    h = _rms(x1, lpre_ref[...]).astype(BF16)
    tm = x1.shape[0]
    acc = jnp.zeros(x1.shape, F32)
    for c in range(d_ff // fc):
        halves = []
        for off in (c * fc, d_ff + c * fc):
            cols = slice(off, off + fc)
            up = _dot(h, wup_ref[:, cols])
            prev = [prev_ref[s, :, cols] for s in range(n_seq)]
            d1, d2 = _shift_rows(up, prev, n_seq, seq_rows)
            cw = cw_ref[:, cols]
            halves.append(cb_ref[:, cols] + cw[0:1] * d2 + cw[1:2] * d1 + cw[2:3] * up)
            for s in range(n_seq):
                last = (s + 1) * seq_rows
                prev_ref[s, :, cols] = up[last - 2:last, :]
        g, u = halves
        gelu = 0.5 * g * (1.0 + jnp.tanh(math.sqrt(2.0 / math.pi) * (g + 0.044715 * (g * g * g))))
        acc = acc + _dot((gelu * u).astype(BF16), wdn_ref[c * fc:(c + 1) * fc, :])
    y_ref[...] = x1 + _rms(acc, lout_ref[...])
    cnew_ref[...] = prev_ref[...]


def _ffn(x2d, og2d, od2d, B, T, w_out, lpost, lpre, lout, w_up, conv_w, conv_b, conv_state, w_down):
    n_rows, D = x2d.shape
    f2 = w_up.shape[1]
    d_ff = f2 // 2
    if T >= ROW_TILE:
        tm, n_seq, seq_rows, tps = ROW_TILE, 1, ROW_TILE, T // ROW_TILE
    else:
        tm, n_seq, seq_rows, tps = n_rows, B, T, 1
    row = lambda w: pl.BlockSpec((tm, w), lambda i: (i, 0))
    cs = pl.BlockSpec((n_seq, CONV_W - 1, f2), lambda i: (i // tps, 0, 0))
    return pl.pallas_call(
        functools.partial(_ffn_kernel, n_seq=n_seq, seq_rows=seq_rows, tiles_per_seq=tps, d_ff=d_ff, fc=256),
        grid=(n_rows // tm,),
        in_specs=[row(D), row(GV), row(DA), _const_spec(w_out.shape), _const_spec((1, D)), _const_spec((1, D)),
                  _const_spec((1, D)), _const_spec(w_up.shape), _const_spec(conv_w.shape),
                  _const_spec((1, f2)), cs, _const_spec(w_down.shape)],
        out_specs=[row(D), cs],
        out_shape=[jax.ShapeDtypeStruct((n_rows, D), F32),
                   jax.ShapeDtypeStruct((B, CONV_W - 1, f2), F32)],
        scratch_shapes=[pltpu.VMEM((n_seq, CONV_W - 1, f2), F32)],
        compiler_params=_params(1),
        name="ffn",
    )(x2d, og2d, od2d, w_out, lpost, lpre, lout, w_up, conv_w, conv_b, conv_state, w_down)


def _lambda_init(layer):
    return 0.8 - 0.6 * math.exp(-0.3 * layer)


def _layer(x2d, B, T, layer, past_k, past_v, gla_state, conv_state, bias, wts):
    (w_main, w_gd, w_gu, b_gu, gn, lp, dn, w_out, ln_pre, ln_post, lf_pre, lf_post,
     w_up, conv_w, conv_b, w_down) = wts
    prompt = past_k is None
    outs = _proj(x2d, B, T, ln_pre, w_main, w_gd, w_gu, b_gu, emit_bf16=prompt)
    gq, gk, gv, gr, la, dq, dk, dv = outs[:8]
    og, s_new = _gla(gq, gk, gv, gr, la, gla_state, gn, B, T)
    lam0 = _lambda_init(layer)
    if prompt:
        od = _attn_prompt(lp, dq, outs[8], outs[9], bias, dn, lam0)
    else:
        od = _attn_step(lp, dq, past_k, past_v, dk, dv, bias, dn, lam0)
    y, c_new = _ffn(x2d, og.reshape(B * T, GV), od.reshape(B * T, DA), B, T, w_out, ln_post, lf_pre, lf_post,
                    w_up, conv_w, conv_b, conv_state, w_down)
    return y, dk, dv, s_new, c_new


def kernel(x_prompt, x_sample, cache_k, cache_v, state_gla, state_conv, t5_table, w_in, w_gate_up, b_gate_up,
           gla_norm, lam_params, diff_norm, w_out, ln_mix_pre, ln_mix_post, ln_ffn_pre, ln_ffn_post,
           w_ffn_up, conv_w, conv_b, w_ffn_down):
    depth = w_in.shape[0]
    B, T, D = x_prompt.shape
    Bs, Ts, _ = x_sample.shape
    past = cache_k.shape[3]
    f2 = w_ffn_up.shape[2]
    gd0 = 2 * GQ + 2 * GV

    def layer_weights(l):
        row = lambda a: a[l].reshape(1, -1)
        w_main = jnp.concatenate([w_in[l, :, :gd0], w_in[l, :, gd0 + GATE_RANK:]], axis=1).astype(BF16)
        return (w_main, w_in[l, :, gd0:gd0 + GATE_RANK].astype(BF16), w_gate_up[l].astype(BF16),
                row(b_gate_up), row(gla_norm), lam_params[l], row(diff_norm), w_out[l].astype(BF16),
                row(ln_mix_pre), row(ln_mix_post), row(ln_ffn_pre), row(ln_ffn_post),
                w_ffn_up[l].astype(BF16), conv_w[l], row(conv_b), w_ffn_down[l].astype(BF16))

    wts = [layer_weights(l) for l in range(depth)]
    tile = min(ATTN_TILE, T)
    bias_prompt = _bias_tiles(t5_table, [0, tile, 2 * tile], tile, tile)
    bias_step = _bias_tiles(t5_table, [past], Ts, past + Ts)

    y = x_prompt.reshape(B * T, D)
    kp, vp, gp, cp = [], [], [], []
    zero_s = jnp.zeros((B, H_G, DK_G, DV_G), F32)
    zero_c = jnp.zeros((B, CONV_W - 1, f2), F32)
    for l in range(depth):
        y, k_new, v_new, s_new, c_new = _layer(y, B, T, l, None, None, zero_s, zero_c, bias_prompt, wts[l])
        kp.append(k_new); vp.append(v_new); gp.append(s_new); cp.append(c_new)
    y_prompt = y.reshape(B, T, D)

    y = x_sample.reshape(Bs * Ts, D)
    ks, vs, gs, cs = [], [], [], []
    for l in range(depth):
        y, k_new, v_new, s_new, c_new = _layer(y, Bs, Ts, l, cache_k[l], cache_v[l], state_gla[l],
                                               state_conv[l], bias_step, wts[l])
        ks.append(k_new); vs.append(v_new); gs.append(s_new); cs.append(c_new)
    y_sample = y.reshape(Bs, Ts, D)

    return (y_prompt, y_sample, jnp.stack(kp), jnp.stack(vp), jnp.stack(gp), jnp.stack(cp),
            jnp.stack(ks), jnp.stack(vs), jnp.stack(gs), jnp.stack(cs))
```

```python
import functools
import math

import jax
import jax.numpy as jnp
from jax import lax
from jax.experimental import pallas as pl
from jax.experimental.pallas import tpu as pltpu

F32 = jnp.float32
BF16 = jnp.bfloat16

CHUNK = 64
H_G = 4
DK_G = 64
DV_G = 128
GATE_RANK = 16
GATE_NORMALIZER = 16.0
H_D = 4
DH_D = 64
NUM_BUCKETS = 32
MAX_DISTANCE = 128
CONV_W = 3
EPS = 1e-6

GQ = H_G * DK_G
GV = H_G * DV_G
DA = H_D * 2 * DH_D
HD2 = 2 * DH_D

VMEM_LIMIT_BYTES = 56 * 1024 * 1024
ROW_TILE = 512
GLA_ROWS = 256
ATTN_TILE = 512
NEG_INIT = -1e30
LOG2E = math.log2(math.e)


def _rms(x, w):
    return x * lax.rsqrt(jnp.mean(x * x, axis=-1, keepdims=True) + EPS) * w


def _dot(a, b):
    return jnp.dot(a, b, preferred_element_type=F32)


def _dot_nt(a, b):
    return lax.dot_general(a, b, (((1,), (1,)), ((), ())), preferred_element_type=F32)


def _params(n_grid):
    return pltpu.CompilerParams(dimension_semantics=("arbitrary",) * n_grid,
                                vmem_limit_bytes=VMEM_LIMIT_BYTES)


def _const_spec(shape):
    nd = len(shape)
    return pl.BlockSpec(shape, lambda *_: (0,) * nd, pipeline_mode=pl.Buffered(1))


def _bias_kernel(tab_ref, o_ref, *, qstarts, nq, nk, keys_on_rows, exp2_shifted):
    h = pl.program_id(0)
    nb = NUM_BUCKETS // 2
    max_exact = nb // 2
    shape = (nk, nq) if keys_on_rows else (nq, nk)
    q_axis, k_axis = (1, 0) if keys_on_rows else (0, 1)
    for d, q0 in enumerate(qstarts):
        qpos = q0 + lax.broadcasted_iota(jnp.int32, shape, q_axis)
        kpos = lax.broadcasted_iota(jnp.int32, shape, k_axis)
        rel = kpos - qpos
        ret = jnp.where(rel > 0, nb, 0)
        n = jnp.abs(rel)
        nf = jnp.maximum(n, 1).astype(F32)
        large = max_exact + (jnp.log(nf / max_exact) / math.log(MAX_DISTANCE / max_exact)
                             * (nb - max_exact)).astype(jnp.int32)
        large = jnp.minimum(large, nb - 1)
        bucket = ret + jnp.where(n < max_exact, n, large)
        bias = jnp.zeros(shape, F32)
        for b in range(NUM_BUCKETS):
            bias = jnp.where(bucket == b, tab_ref[b, h], bias)
        if exp2_shifted:
            bias = (bias - tab_ref[nb - 1, h]) * LOG2E
        visible = (kpos // CHUNK) <= (qpos // CHUNK)
        o_ref[0, d] = jnp.where(visible, bias, -jnp.inf)


def _bias_tiles(t5_table, qstarts, nq, nk, *, keys_on_rows=False, exp2_shifted=False):
    nd = len(qstarts)
    shape = (nk, nq) if keys_on_rows else (nq, nk)
    return pl.pallas_call(
        functools.partial(_bias_kernel, qstarts=tuple(qstarts), nq=nq, nk=nk, keys_on_rows=keys_on_rows,
                          exp2_shifted=exp2_shifted),
        grid=(H_D,),
        in_specs=[pl.BlockSpec(memory_space=pltpu.SMEM)],
        out_specs=pl.BlockSpec((1, nd) + shape, lambda h: (h, 0, 0, 0)),
        out_shape=jax.ShapeDtypeStruct((H_D, nd) + shape, F32),
        compiler_params=_params(1),
        name="t5_bias_tiles",
    )(t5_table)


def _proj_kernel(x_ref, ln_ref, w_ref, wgd_ref, wgu_ref, bgu_ref,
                 gq_ref, gk_ref, gv_ref, gr_ref, la_ref, dq_ref, dk_ref, dv_ref, *bf_refs,
                 n_seq, seq_rows):
    h = _rms(x_ref[...], ln_ref[...]).astype(BF16)

    def seg(a, b):
        return _dot(h, w_ref[:, a:b])

    gq_ref[...] = seg(0, GQ) * DK_G ** -0.5
    gk_ref[...] = seg(GQ, 2 * GQ)
    gv_ref[...] = seg(2 * GQ, 2 * GQ + GV)
    gr_ref[...] = seg(2 * GQ + GV, 2 * GQ + 2 * GV)
    gd = _dot(h, wgd_ref[...]).astype(BF16)
    z = _dot(gd, wgu_ref[...]) + bgu_ref[...]
    log_sig = jnp.minimum(z, 0.0) - jnp.log1p(jnp.exp(-jnp.abs(z)))
    la_ref[...] = log_sig / GATE_NORMALIZER

    base = 2 * GQ + 2 * GV
    q = seg(base, base + DA) * DH_D ** -0.5
    k = seg(base + DA, base + 2 * DA)
    v = seg(base + 2 * DA, base + 3 * DA)
    for s in range(n_seq):
        rows = slice(s * seq_rows, (s + 1) * seq_rows)
        for hh in range(H_D):
            cols = slice(hh * HD2, (hh + 1) * HD2)
            dq_ref[s, hh] = q[rows, cols]
            dk_ref[s, hh] = k[rows, cols]
            dv_ref[s, hh] = v[rows, cols]
            if bf_refs:
                bf_refs[0][s, hh] = k[rows, cols].astype(BF16)
                for kb in range(seq_rows // ATTN_TILE):
                    r0 = s * seq_rows + kb * ATTN_TILE
                    bf_refs[1][s, hh, kb] = v[r0:r0 + ATTN_TILE, cols].T.astype(BF16)


def _proj(x2d, B, T, ln, w_main, w_gd, w_gu, b_gu, *, emit_bf16):
    n_rows, D = x2d.shape
    if T >= ROW_TILE:
        tm, n_seq, seq_rows, tps = ROW_TILE, 1, ROW_TILE, T // ROW_TILE
    else:
        tm, n_seq, seq_rows, tps = n_rows, B, T, 1
    grid = (n_rows // tm,)
    row = lambda w: pl.BlockSpec((tm, w), lambda i: (i, 0))
    head = pl.BlockSpec((n_seq, H_D, seq_rows, HD2), lambda i: (i // tps, 0, i % tps, 0))
    f32_rows = lambda w: jax.ShapeDtypeStruct((n_rows, w), F32)
    head_shape = lambda dt: jax.ShapeDtypeStruct((B, H_D, T, HD2), dt)
    out_specs = [row(GQ), row(GQ), row(GV), row(GV), row(GQ), head, head, head]
    out_shape = [f32_rows(GQ), f32_rows(GQ), f32_rows(GV), f32_rows(GV), f32_rows(GQ),
                 head_shape(F32), head_shape(F32), head_shape(F32)]
    if emit_bf16:
        kt = seq_rows // ATTN_TILE
        out_specs += [head, pl.BlockSpec((n_seq, H_D, kt, HD2, ATTN_TILE), lambda i: (i // tps, 0, i % tps, 0, 0))]
        out_shape += [head_shape(BF16), jax.ShapeDtypeStruct((B, H_D, T // ATTN_TILE, HD2, ATTN_TILE), BF16)]
    return pl.pallas_call(
        functools.partial(_proj_kernel, n_seq=n_seq, seq_rows=seq_rows),
        grid=grid,
        in_specs=[row(D), _const_spec((1, D)), _const_spec(w_main.shape), _const_spec(w_gd.shape),
                  _const_spec(w_gu.shape), _const_spec((1, GQ))],
        out_specs=out_specs,
        out_shape=out_shape,
        compiler_params=_params(1),
        name="proj",
    )(x2d, ln, w_main, w_gd, w_gu, b_gu)


def _gla_kernel(q_ref, k_ref, v_ref, r_ref, la_ref, s0_ref, gn_ref, o_ref, sout_ref, s_ref, *, n_chunks):
    t = pl.program_id(1)
    srow = lax.broadcasted_iota(jnp.int32, (GQ, GV), 0) // DK_G
    scol = lax.broadcasted_iota(jnp.int32, (GQ, GV), 1) // DV_G
    on_diag = srow == scol

    @pl.when(t == 0)
    def _():
        s_ref[...] = jnp.zeros((GQ, GV), F32)
        for hh in range(H_G):
            s_ref[hh * DK_G:(hh + 1) * DK_G, hh * DV_G:(hh + 1) * DV_G] = s0_ref[0, hh]

    ri = lax.broadcasted_iota(jnp.int32, (CHUNK, CHUNK), 0)
    ci = lax.broadcasted_iota(jnp.int32, (CHUNK, CHUNK), 1)
    causal = ci <= ri
    tril = causal.astype(BF16)
    head_of_lane = lax.broadcasted_iota(jnp.int32, (CHUNK, GQ), 1) // DK_G
    gn = gn_ref[...]

    for c in range(n_chunks):
        rows = slice(c * CHUNK, (c + 1) * CHUNK)
        la = la_ref[0, rows, :]
        a1 = la.astype(BF16)
        r1 = la - a1.astype(F32)
        a2 = r1.astype(BF16)
        a3 = (r1 - a2.astype(F32)).astype(BF16)
        g = _dot(tril, a1) + _dot(tril, a2) + _dot(tril, a3)
        g_last = g[CHUNK - 1:CHUNK, :]
        q = q_ref[0, rows, :]
        k = k_ref[0, rows, :]
        v = v_ref[0, rows, :].astype(BF16)
        q_dec = q * jnp.exp(g)
        k_inv = (k * jnp.exp(-g)).astype(BF16)
        k_end = k * jnp.exp(g_last - g)
        decay_col = jnp.exp(jnp.sum(la.T, axis=1, keepdims=True))

        s_old = s_ref[...]
        o_inter = _dot(q_dec.astype(BF16), s_old.astype(BF16))
        upd = _dot(k_end.T.astype(BF16), v)
        s_ref[...] = decay_col * s_old + jnp.where(on_diag, upd, 0.0)

        for hh in range(H_G):
            vcols = slice(hh * DV_G, (hh + 1) * DV_G)
            qm = jnp.where(head_of_lane == hh, q_dec, 0.0).astype(BF16)
            a = jnp.where(causal, _dot_nt(qm, k_inv), 0.0)
            o = _dot(a.astype(BF16), v[:, vcols]) + o_inter[:, vcols]
            r = r_ref[0, rows, vcols]
            o_ref[0, rows, vcols] = _rms(o, gn) * (r * (1.0 / (1.0 + jnp.exp(-r))))

    for hh in range(H_G):
        sout_ref[0, hh] = s_ref[hh * DK_G:(hh + 1) * DK_G, hh * DV_G:(hh + 1) * DV_G]


def _gla(gq, gk, gv, gr, la, s0, gn, B, T):
    tb = min(GLA_ROWS, T)
    r3 = lambda a: a.reshape(B, T, a.shape[-1])
    blk = lambda w: pl.BlockSpec((1, tb, w), lambda b, t: (b, t, 0))
    st = pl.BlockSpec((1, H_G, DK_G, DV_G), lambda b, t: (b, 0, 0, 0))
    return pl.pallas_call(
        functools.partial(_gla_kernel, n_chunks=tb // CHUNK),
        grid=(B, T // tb),
        in_specs=[blk(GQ), blk(GQ), blk(GV), blk(GV), blk(GQ), st, _const_spec((1, DV_G))],
        out_specs=[blk(GV), st],
        out_shape=[jax.ShapeDtypeStruct((B, T, GV), F32),
                   jax.ShapeDtypeStruct((B, H_G, DK_G, DV_G), F32)],
        scratch_shapes=[pltpu.VMEM((GQ, GV), F32)],
        compiler_params=_params(2),
        name="gla",
    )(r3(gq), r3(gk), r3(gv), r3(gr), r3(la), s0, gn)


def _lam(lp_ref, lam0):
    lp = lp_ref[...]
    t1 = jnp.sum(lp[0:1] * lp[1:2], axis=-1, keepdims=True)
    t2 = jnp.sum(lp[2:3] * lp[3:4], axis=-1, keepdims=True)
    return jnp.exp(t1) - jnp.exp(t2) + lam0


def _split_maps(q):
    lane = lax.broadcasted_iota(jnp.int32, q.shape, 1)
    return (jnp.where(lane < DH_D, q, 0.0).astype(BF16),
            jnp.where(lane >= DH_D, q, 0.0).astype(BF16))


def _attn_finish(acc0, l0, acc1, l1, lam, dn, lam0):
    o = acc0 * (1.0 / l0) - lam * (acc1 * (1.0 / l1))
    return _rms(o, dn) * (1.0 - lam0)


def _attn_prompt_kernel(lp_ref, q_ref, k_ref, vt_ref, bias_ref, dn_ref, o_ref,
                        qm_ref, s_ref, p_ref, alpha_ref, m_ref, l_ref, acc_ref, *, tile, lam0):
    i = pl.program_id(2)
    qt = (q_ref[0, 0] * LOG2E).T
    sub = lax.broadcasted_iota(jnp.int32, qt.shape, 0)
    qm_ref[0] = jnp.where(sub < DH_D, qt, 0.0).astype(BF16)
    qm_ref[1] = jnp.where(sub >= DH_D, qt, 0.0).astype(BF16)
    m_ref[...] = jnp.full(m_ref.shape, NEG_INIT, F32)
    l_ref[...] = jnp.zeros(l_ref.shape, F32)
    acc_ref[...] = jnp.zeros(acc_ref.shape, F32)

    def scores(j, slot):
        ks = pl.multiple_of(j * tile, tile)
        kt = k_ref[0, 0, pl.ds(ks, tile), :]
        for m in range(2):
            s_ref[slot, m] = _dot(kt, qm_ref[m])

    def weighted_values(j, slot):
        vt = vt_ref[0, 0, j]
        for m in range(2):
            acc_ref[m] = alpha_ref[slot, m] * acc_ref[m] + _dot(vt, p_ref[slot, m])

    def softmax_step(slot, near):
        for m in range(2):
            s = s_ref[slot, m]
            if near is not None:
                s = s + bias_ref[0, near]
            m_old = m_ref[m]
            m_new = jnp.maximum(m_old, jnp.max(s, axis=0, keepdims=True))
            p = jnp.exp2(s - m_new)
            p_ref[slot, m] = p.astype(BF16)
            alpha = jnp.exp2(m_old - m_new)
            alpha_ref[slot, m] = alpha
            l_ref[m] = alpha * l_ref[m] + jnp.sum(p, axis=0, keepdims=True)
            m_ref[m] = m_new

    def step(r, slot, *, near=None, values=True, next_scores=True):
        if next_scores:
            scores(jnp.maximum(i - r - 1, 0), 1 - slot)
        if values:
            weighted_values(i - r + 1, 1 - slot)
        softmax_step(slot, near)

    scores(i, 0)
    step(0, 0, near=0, values=False)

    @pl.when(i >= 1)
    def _():
        step(1, 1, near=1)

    def far_pair(c, carry):
        r = 2 * c + 2
        step(r, 0)
        step(r + 1, 1)
        return carry

    lax.fori_loop(0, jnp.maximum(i - 1, 0) // 2, far_pair, 0)

    @pl.when(jnp.logical_and(i >= 2, i % 2 == 0))
    def _():
        step(i, 0, next_scores=False)

    @pl.when(i % 2 == 0)
    def _():
        weighted_values(0, 0)

    @pl.when(i % 2 == 1)
    def _():
        weighted_values(0, 1)

    lam = _lam(lp_ref, lam0)
    ot = acc_ref[0] * (1.0 / l_ref[0]) - lam * (acc_ref[1] * (1.0 / l_ref[1]))
    o_ref[0] = _rms(ot.T, dn_ref[...]) * (1.0 - lam0)


def _attn_prompt(lp, dq, kb, vtb, bias, dn, lam0):
    B, H, T, _ = dq.shape
    tile = ATTN_TILE
    return pl.pallas_call(
        functools.partial(_attn_prompt_kernel, tile=tile, lam0=lam0),
        grid=(B, H, T // tile),
        in_specs=[_const_spec(lp.shape),
                  pl.BlockSpec((1, 1, tile, HD2), lambda b, h, i: (b, h, i, 0)),
                  pl.BlockSpec((1, 1, T, HD2), lambda b, h, i: (b, h, 0, 0)),
                  pl.BlockSpec((1, 1, T // tile, HD2, tile), lambda b, h, i: (b, h, 0, 0, 0)),
                  pl.BlockSpec((1, 2, tile, tile), lambda b, h, i: (h, 0, 0, 0)),
                  _const_spec((1, HD2))],
        out_specs=pl.BlockSpec((1, tile, HD2), lambda b, h, i: (b, i, h)),
        out_shape=jax.ShapeDtypeStruct((B, T, H * HD2), F32),
        scratch_shapes=[pltpu.VMEM((2, HD2, tile), BF16),
                        pltpu.VMEM((2, 2, tile, tile), F32),
                        pltpu.VMEM((2, 2, tile, tile), BF16),
                        pltpu.VMEM((2, 2, 1, tile), F32),
                        pltpu.VMEM((2, 1, tile), F32), pltpu.VMEM((2, 1, tile), F32),
                        pltpu.VMEM((2, HD2, tile), F32)],
        compiler_params=_params(3),
        name="attn_prompt",
    )(lp, dq, kb, vtb, bias, dn)


def _attn_step_kernel(lp_ref, q_ref, kp_ref, vp_ref, kn_ref, vn_ref, bias_ref, dn_ref, o_ref, *, past, lam0):
    qm = _split_maps(q_ref[0, 0])
    kp = kp_ref[0, 0].astype(BF16)
    vp = vp_ref[0, 0].astype(BF16)
    kn = kn_ref[0, 0].astype(BF16)
    vn = vn_ref[0, 0].astype(BF16)
    bias_p = bias_ref[0, 0, :, :past]
    bias_n = bias_ref[0, 0, :, past:]
    res = []
    for m in range(2):
        s_p = _dot_nt(qm[m], kp) + bias_p
        s_n = _dot_nt(qm[m], kn) + bias_n
        mx = jnp.maximum(jnp.max(s_p, axis=-1, keepdims=True), jnp.max(s_n, axis=-1, keepdims=True))
        p_p = jnp.exp(s_p - mx)
        p_n = jnp.exp(s_n - mx)
        l = jnp.sum(p_p, axis=-1, keepdims=True) + jnp.sum(p_n, axis=-1, keepdims=True)
        acc = _dot(p_p.astype(BF16), vp) + _dot(p_n.astype(BF16), vn)
        res += [acc, l]
    o_ref[0] = _attn_finish(res[0], res[1], res[2], res[3], _lam(lp_ref, lam0), dn_ref[...], lam0)


def _attn_step(lp, dq, k_past, v_past, dk, dv, bias, dn, lam0):
    B, H, T, _ = dq.shape
    past = k_past.shape[2]
    new = pl.BlockSpec((1, 1, T, HD2), lambda b, h: (b, h, 0, 0))
    old = pl.BlockSpec((1, 1, past, HD2), lambda b, h: (b, h, 0, 0))
    return pl.pallas_call(
        functools.partial(_attn_step_kernel, past=past, lam0=lam0),
        grid=(B, H),
        in_specs=[_const_spec(lp.shape), new, old, old, new, new,
                  pl.BlockSpec((1, 1, T, past + T), lambda b, h: (h, 0, 0, 0)),
                  _const_spec((1, HD2))],
        out_specs=pl.BlockSpec((1, T, HD2), lambda b, h: (b, 0, h)),
        out_shape=jax.ShapeDtypeStruct((B, T, H * HD2), F32),
        compiler_params=_params(2),
        name="attn_step",
    )(lp, dq, k_past, v_past, dk, dv, bias, dn)


def _shift_rows(up, prev, n_seq, seq_rows):
    row = lax.broadcasted_iota(jnp.int32, up.shape, 0)
    d1 = pltpu.roll(up, 1, 0)
    d2 = pltpu.roll(up, 2, 0)
    for s in range(n_seq):
        first = s * seq_rows
        d1 = jnp.where(row == first, prev[s][1:2], d1)
        d2 = jnp.where(row == first, prev[s][0:1], d2)
        d2 = jnp.where(row == first + 1, prev[s][1:2], d2)
    return d1, d2


def _ffn_kernel(x_ref, og_ref, od_ref, wo_ref, lpost_ref, lpre_ref, lout_ref, wup_ref, cw_ref, cb_ref,
                cs_ref, wdn_ref, y_ref, cnew_ref, prev_ref, *, n_seq, seq_rows, tiles_per_seq, d_ff, fc):
    ti = pl.program_id(0) % tiles_per_seq

    @pl.when(ti == 0)
    def _():
        prev_ref[...] = cs_ref[...]

    mix = _dot(og_ref[...].astype(BF16), wo_ref[:GV, :]) + _dot(od_ref[...].astype(BF16), wo_ref[GV:, :])
    x1 = x_ref[...] + _rms(mix, lpost_ref[...])
    h = _rms(x1, lpre_ref[...]).astype(BF16)
    tm = x1.shape[0]
    acc = jnp.zeros(x1.shape, F32)
    for c in range(d_ff // fc):
        halves = []
        for off in (c * fc, d_ff + c * fc):
            cols = slice(off, off + fc)
            up = _dot(h, wup_ref[:, cols])
            prev = [prev_ref[s, :, cols] for s in range(n_seq)]
            d1, d2 = _shift_rows(up, prev, n_seq, seq_rows)
            cw = cw_ref[:, cols]
            halves.append(cb_ref[:, cols] + cw[0:1] * d2 + cw[1:2] * d1 + cw[2:3] * up)
            for s in range(n_seq):
                last = (s + 1) * seq_rows
                prev_ref[s, :, cols] = up[last - 2:last, :]
        g, u = halves
        gelu = 0.5 * g * (1.0 + jnp.tanh(math.sqrt(2.0 / math.pi) * (g + 0.044715 * (g * g * g))))
        acc = acc + _dot((gelu * u).astype(BF16), wdn_ref[c * fc:(c + 1) * fc, :])
    y_ref[...] = x1 + _rms(acc, lout_ref[...])
    cnew_ref[...] = prev_ref[...]


def _ffn(x2d, og2d, od2d, B, T, w_out, lpost, lpre, lout, w_up, conv_w, conv_b, conv_state, w_down):
    n_rows, D = x2d.shape
    f2 = w_up.shape[1]
    d_ff = f2 // 2
    if T >= ROW_TILE:
        tm, n_seq, seq_rows, tps = ROW_TILE, 1, ROW_TILE, T // ROW_TILE
    else:
        tm, n_seq, seq_rows, tps = n_rows, B, T, 1
    row = lambda w: pl.BlockSpec((tm, w), lambda i: (i, 0))
    cs = pl.BlockSpec((n_seq, CONV_W - 1, f2), lambda i: (i // tps, 0, 0))
    return pl.pallas_call(
        functools.partial(_ffn_kernel, n_seq=n_seq, seq_rows=seq_rows, tiles_per_seq=tps, d_ff=d_ff, fc=256),
        grid=(n_rows // tm,),
        in_specs=[row(D), row(GV), row(DA), _const_spec(w_out.shape), _const_spec((1, D)), _const_spec((1, D)),
                  _const_spec((1, D)), _const_spec(w_up.shape), _const_spec(conv_w.shape),
                  _const_spec((1, f2)), cs, _const_spec(w_down.shape)],
        out_specs=[row(D), cs],
        out_shape=[jax.ShapeDtypeStruct((n_rows, D), F32),
                   jax.ShapeDtypeStruct((B, CONV_W - 1, f2), F32)],
        scratch_shapes=[pltpu.VMEM((n_seq, CONV_W - 1, f2), F32)],
        compiler_params=_params(1),
        name="ffn",
    )(x2d, og2d, od2d, w_out, lpost, lpre, lout, w_up, conv_w, conv_b, conv_state, w_down)


def _lambda_init(layer):
    return 0.8 - 0.6 * math.exp(-0.3 * layer)


def _layer(x2d, B, T, layer, past_k, past_v, gla_state, conv_state, bias, wts):
    (w_main, w_gd, w_gu, b_gu, gn, lp, dn, w_out, ln_pre, ln_post, lf_pre, lf_post,
     w_up, conv_w, conv_b, w_down) = wts
    prompt = past_k is None
    outs = _proj(x2d, B, T, ln_pre, w_main, w_gd, w_gu, b_gu, emit_bf16=prompt)
    gq, gk, gv, gr, la, dq, dk, dv = outs[:8]
    og, s_new = _gla(gq, gk, gv, gr, la, gla_state, gn, B, T)
    lam0 = _lambda_init(layer)
    if prompt:
        od = _attn_prompt(lp, dq, outs[8], outs[9], bias, dn, lam0)
    else:
        od = _attn_step(lp, dq, past_k, past_v, dk, dv, bias, dn, lam0)
    y, c_new = _ffn(x2d, og.reshape(B * T, GV), od.reshape(B * T, DA), B, T, w_out, ln_post, lf_pre, lf_post,
                    w_up, conv_w, conv_b, conv_state, w_down)
    return y, dk, dv, s_new, c_new


def kernel(x_prompt, x_sample, cache_k, cache_v, state_gla, state_conv, t5_table, w_in, w_gate_up, b_gate_up,
           gla_norm, lam_params, diff_norm, w_out, ln_mix_pre, ln_mix_post, ln_ffn_pre, ln_ffn_post,
           w_ffn_up, conv_w, conv_b, w_ffn_down):
    depth = w_in.shape[0]
    B, T, D = x_prompt.shape
    Bs, Ts, _ = x_sample.shape
    past = cache_k.shape[3]
    f2 = w_ffn_up.shape[2]
    gd0 = 2 * GQ + 2 * GV

    def layer_weights(l):
        row = lambda a: a[l].reshape(1, -1)
        w_main = jnp.concatenate([w_in[l, :, :gd0], w_in[l, :, gd0 + GATE_RANK:]], axis=1).astype(BF16)
        return (w_main, w_in[l, :, gd0:gd0 + GATE_RANK].astype(BF16), w_gate_up[l].astype(BF16),
                row(b_gate_up), row(gla_norm), lam_params[l], row(diff_norm), w_out[l].astype(BF16),
                row(ln_mix_pre), row(ln_mix_post), row(ln_ffn_pre), row(ln_ffn_post),
                w_ffn_up[l].astype(BF16), conv_w[l], row(conv_b), w_ffn_down[l].astype(BF16))

    wts = [layer_weights(l) for l in range(depth)]
    bias_prompt = _bias_tiles(t5_table, [0, ATTN_TILE], ATTN_TILE, ATTN_TILE, keys_on_rows=True, exp2_shifted=True)
    bias_step = _bias_tiles(t5_table, [past], Ts, past + Ts)

    y = x_prompt.reshape(B * T, D)
    kp, vp, gp, cp = [], [], [], []
    zero_s = jnp.zeros((B, H_G, DK_G, DV_G), F32)
    zero_c = jnp.zeros((B, CONV_W - 1, f2), F32)
    for l in range(depth):
        y, k_new, v_new, s_new, c_new = _layer(y, B, T, l, None, None, zero_s, zero_c, bias_prompt, wts[l])
        kp.append(k_new); vp.append(v_new); gp.append(s_new); cp.append(c_new)
    y_prompt = y.reshape(B, T, D)

    y = x_sample.reshape(Bs * Ts, D)
    ks, vs, gs, cs = [], [], [], []
    for l in range(depth):
        y, k_new, v_new, s_new, c_new = _layer(y, Bs, Ts, l, cache_k[l], cache_v[l], state_gla[l],
                                               state_conv[l], bias_step, wts[l])
        ks.append(k_new); vs.append(v_new); gs.append(s_new); cs.append(c_new)
    y_sample = y.reshape(Bs, Ts, D)

    return (y_prompt, y_sample, jnp.stack(kp), jnp.stack(vp), jnp.stack(gp), jnp.stack(cp),
            jnp.stack(ks), jnp.stack(vs), jnp.stack(gs), jnp.stack(cs))
```

```python
import functools
import math

import jax
import jax.numpy as jnp
from jax import lax
from jax.experimental import pallas as pl
from jax.experimental.pallas import tpu as pltpu

F32 = jnp.float32
BF16 = jnp.bfloat16

CHUNK = 64
H_G = 4
DK_G = 64
DV_G = 128
GATE_RANK = 16
GATE_NORMALIZER = 16.0
H_D = 4
DH_D = 64
NUM_BUCKETS = 32
MAX_DISTANCE = 128
CONV_W = 3
EPS = 1e-6

GQ = H_G * DK_G
GV = H_G * DV_G
DA = H_D * 2 * DH_D
HD2 = 2 * DH_D

VMEM_LIMIT_BYTES = 56 * 1024 * 1024
ROW_TILE = 512
GLA_ROWS = 256
ATTN_TILE = 512
NEG_INIT = -1e30
LOG2E = math.log2(math.e)


def _rms(x, w):
    return x * lax.rsqrt(jnp.mean(x * x, axis=-1, keepdims=True) + EPS) * w


def _dot(a, b):
    return jnp.dot(a, b, preferred_element_type=F32)


def _dot_nt(a, b):
    return lax.dot_general(a, b, (((1,), (1,)), ((), ())), preferred_element_type=F32)


def _params(n_grid):
    return pltpu.CompilerParams(dimension_semantics=("arbitrary",) * n_grid,
                                vmem_limit_bytes=VMEM_LIMIT_BYTES)


def _const_spec(shape):
    nd = len(shape)
    return pl.BlockSpec(shape, lambda *_: (0,) * nd, pipeline_mode=pl.Buffered(1))


def _bias_kernel(tab_ref, o_ref, *, qstarts, nq, nk, keys_on_rows, exp2_shifted):
    h = pl.program_id(0)
    nb = NUM_BUCKETS // 2
    max_exact = nb // 2
    shape = (nk, nq) if keys_on_rows else (nq, nk)
    q_axis, k_axis = (1, 0) if keys_on_rows else (0, 1)
    for d, q0 in enumerate(qstarts):
        qpos = q0 + lax.broadcasted_iota(jnp.int32, shape, q_axis)
        kpos = lax.broadcasted_iota(jnp.int32, shape, k_axis)
        rel = kpos - qpos
        ret = jnp.where(rel > 0, nb, 0)
        n = jnp.abs(rel)
        nf = jnp.maximum(n, 1).astype(F32)
        large = max_exact + (jnp.log(nf / max_exact) / math.log(MAX_DISTANCE / max_exact)
                             * (nb - max_exact)).astype(jnp.int32)
        large = jnp.minimum(large, nb - 1)
        bucket = ret + jnp.where(n < max_exact, n, large)
        bias = jnp.zeros(shape, F32)
        for b in range(NUM_BUCKETS):
            bias = jnp.where(bucket == b, tab_ref[b, h], bias)
        if exp2_shifted:
            bias = (bias - tab_ref[nb - 1, h]) * LOG2E
        visible = (kpos // CHUNK) <= (qpos // CHUNK)
        o_ref[0, d] = jnp.where(visible, bias, -jnp.inf)


def _bias_tiles(t5_table, qstarts, nq, nk, *, keys_on_rows=False, exp2_shifted=False):
    nd = len(qstarts)
    shape = (nk, nq) if keys_on_rows else (nq, nk)
    return pl.pallas_call(
        functools.partial(_bias_kernel, qstarts=tuple(qstarts), nq=nq, nk=nk, keys_on_rows=keys_on_rows,
                          exp2_shifted=exp2_shifted),
        grid=(H_D,),
        in_specs=[pl.BlockSpec(memory_space=pltpu.SMEM)],
        out_specs=pl.BlockSpec((1, nd) + shape, lambda h: (h, 0, 0, 0)),
        out_shape=jax.ShapeDtypeStruct((H_D, nd) + shape, F32),
        compiler_params=_params(1),
        name="t5_bias_tiles",
    )(t5_table)


def _proj_kernel(*refs, n_seq, seq_rows, n_alias):
    x_ref, ln_ref, w_ref, wgd_ref, wgu_ref, bgu_ref = refs[:6]
    gq_ref, gk_ref, gv_ref, gr_ref, la_ref, dq_ref, dk_ref, dv_ref = refs[6 + n_alias:14 + n_alias]
    bf_refs = refs[14 + n_alias:]
    h = _rms(x_ref[...], ln_ref[...]).astype(BF16)

    def seg(a, b):
        return _dot(h, w_ref[:, a:b])

    gq_ref[...] = seg(0, GQ) * DK_G ** -0.5
    gk_ref[...] = seg(GQ, 2 * GQ)
    gv_ref[...] = seg(2 * GQ, 2 * GQ + GV)
    gr_ref[...] = seg(2 * GQ + GV, 2 * GQ + 2 * GV)
    gd = _dot(h, wgd_ref[...]).astype(BF16)
    z = _dot(gd, wgu_ref[...]) + bgu_ref[...]
    log_sig = jnp.minimum(z, 0.0) - jnp.log1p(jnp.exp(-jnp.abs(z)))
    la_ref[...] = log_sig / GATE_NORMALIZER

    base = 2 * GQ + 2 * GV
    q = seg(base, base + DA) * DH_D ** -0.5
    k = seg(base + DA, base + 2 * DA)
    v = seg(base + 2 * DA, base + 3 * DA)
    for s in range(n_seq):
        rows = slice(s * seq_rows, (s + 1) * seq_rows)
        for hh in range(H_D):
            cols = slice(hh * HD2, (hh + 1) * HD2)
            dq_ref[s, hh] = q[rows, cols]
            dk_ref[s, hh] = k[rows, cols]
            dv_ref[s, hh] = v[rows, cols]
            if bf_refs:
                bf_refs[0][s, hh] = k[rows, cols].astype(BF16)
                for kb in range(seq_rows // ATTN_TILE):
                    r0 = s * seq_rows + kb * ATTN_TILE
                    bf_refs[1][s, hh, kb] = v[r0:r0 + ATTN_TILE, cols].T.astype(BF16)


def _proj(x2d, B, T, ln, w_main, w_gd, w_gu, b_gu, *, layer, depth, kv_slabs, emit_bf16):
    n_rows, D = x2d.shape
    if T >= ROW_TILE:
        tm, n_seq, seq_rows, tps = ROW_TILE, 1, ROW_TILE, T // ROW_TILE
    else:
        tm, n_seq, seq_rows, tps = n_rows, B, T, 1
    grid = (n_rows // tm,)
    row = lambda w: pl.BlockSpec((tm, w), lambda i: (i, 0))
    head = pl.BlockSpec((n_seq, H_D, seq_rows, HD2), lambda i: (i // tps, 0, i % tps, 0))
    slab = pl.BlockSpec((None, n_seq, H_D, seq_rows, HD2), lambda i: (layer, i // tps, 0, i % tps, 0))
    f32_rows = lambda w: jax.ShapeDtypeStruct((n_rows, w), F32)
    head_shape = lambda dt: jax.ShapeDtypeStruct((B, H_D, T, HD2), dt)
    slab_shape = jax.ShapeDtypeStruct((depth, B, H_D, T, HD2), F32)
    out_specs = [row(GQ), row(GQ), row(GV), row(GV), row(GQ), head, slab, slab]
    out_shape = [f32_rows(GQ), f32_rows(GQ), f32_rows(GV), f32_rows(GV), f32_rows(GQ),
                 head_shape(F32), slab_shape, slab_shape]
    if emit_bf16:
        kt = seq_rows // ATTN_TILE
        out_specs += [head, pl.BlockSpec((n_seq, H_D, kt, HD2, ATTN_TILE), lambda i: (i // tps, 0, i % tps, 0, 0))]
        out_shape += [head_shape(BF16), jax.ShapeDtypeStruct((B, H_D, T // ATTN_TILE, HD2, ATTN_TILE), BF16)]
    args = [x2d, ln, w_main, w_gd, w_gu, b_gu]
    in_specs = [row(D), _const_spec((1, D)), _const_spec(w_main.shape), _const_spec(w_gd.shape),
                _const_spec(w_gu.shape), _const_spec((1, GQ))]
    aliases = {}
    if kv_slabs is not None:
        aliases = {len(args): 6, len(args) + 1: 7}
        args += list(kv_slabs)
        in_specs += [pl.BlockSpec(memory_space=pl.ANY)] * 2
    return pl.pallas_call(
        functools.partial(_proj_kernel, n_seq=n_seq, seq_rows=seq_rows, n_alias=len(aliases)),
        grid=grid,
        in_specs=in_specs,
        out_specs=out_specs,
        out_shape=out_shape,
        input_output_aliases=aliases,
        compiler_params=_params(1),
        name="proj",
    )(*args)


def _gla_kernel(q_ref, k_ref, v_ref, r_ref, la_ref, s0_ref, gn_ref, o_ref, sout_ref, s_ref, *, n_chunks):
    t = pl.program_id(1)

    @pl.when(t == 0)
    def _():
        for hh in range(H_G):
            s_ref[hh * DK_G:(hh + 1) * DK_G, :] = s0_ref[0, hh]

    R = n_chunks * CHUNK
    ri = lax.broadcasted_iota(jnp.int32, (R, R), 0)
    ci = lax.broadcasted_iota(jnp.int32, (R, R), 1)
    causal = jnp.logical_and((ri // CHUNK) == (ci // CHUNK), ci <= ri)
    head_of_lane = lax.broadcasted_iota(jnp.int32, (R, GQ), 1) // DK_G

    la = la_ref[0]
    a1 = la.astype(BF16)
    r1 = la - a1.astype(F32)
    a2 = r1.astype(BF16)
    a3 = (r1 - a2.astype(F32)).astype(BF16)
    tril = causal.astype(BF16)
    g = _dot(tril, a1) + _dot(tril, a2) + _dot(tril, a3)
    chunk_rows = [slice(c * CHUNK, (c + 1) * CHUNK) for c in range(n_chunks)]
    last_rows = [g[(c + 1) * CHUNK - 1:(c + 1) * CHUNK, :] for c in range(n_chunks)]
    g_last = jnp.concatenate([jnp.broadcast_to(x, (CHUNK, GQ)) for x in last_rows], axis=0)
    pad = [jnp.zeros((8 - n_chunks % 8, GQ), F32)] if n_chunks % 8 else []
    decay = jnp.exp(jnp.concatenate(last_rows + pad, axis=0).T)
    q = q_ref[0]
    k = k_ref[0]
    v = v_ref[0].astype(BF16)
    q_dec = q * jnp.exp(g)
    k_inv = (k * jnp.exp(-g)).astype(BF16)
    k_end = k * jnp.exp(g_last - g)
    qm = [jnp.where(head_of_lane == hh, q_dec, 0.0).astype(BF16) for hh in range(H_G)]
    vcols = [slice(hh * DV_G, (hh + 1) * DV_G) for hh in range(H_G)]

    o_inter = [[] for _ in range(H_G)]
    state = s_ref[...]
    for c, rows in enumerate(chunk_rows):
        s_b = state.astype(BF16)
        kt = k_end[rows].T.astype(BF16)
        upd = []
        for hh in range(H_G):
            o_inter[hh].append(_dot(qm[hh][rows], s_b))
            upd.append(_dot(kt[hh * DK_G:(hh + 1) * DK_G, :], v[rows, vcols[hh]]))
        state = decay[:, c:c + 1] * state + jnp.concatenate(upd, axis=0)
    s_ref[...] = state

    gn = gn_ref[...]
    for hh in range(H_G):
        a = jnp.where(causal, _dot_nt(qm[hh], k_inv), 0.0)
        o = _dot(a.astype(BF16), v[:, vcols[hh]]) + jnp.concatenate(o_inter[hh], axis=0)
        r = r_ref[0, :, vcols[hh]]
        o_ref[0, :, vcols[hh]] = _rms(o, gn) * (r * (1.0 / (1.0 + jnp.exp(-r))))

    for hh in range(H_G):
        sout_ref[0, hh] = s_ref[hh * DK_G:(hh + 1) * DK_G, :]


def _gla(gq, gk, gv, gr, la, s0, gn, B, T):
    tb = min(GLA_ROWS, T)
    r3 = lambda a: a.reshape(B, T, a.shape[-1])
    blk = lambda w: pl.BlockSpec((1, tb, w), lambda b, t: (b, t, 0))
    st = pl.BlockSpec((1, H_G, DK_G, DV_G), lambda b, t: (b, 0, 0, 0))
    return pl.pallas_call(
        functools.partial(_gla_kernel, n_chunks=tb // CHUNK),
        grid=(B, T // tb),
        in_specs=[blk(GQ), blk(GQ), blk(GV), blk(GV), blk(GQ), st, _const_spec((1, DV_G))],
        out_specs=[blk(GV), st],
        out_shape=[jax.ShapeDtypeStruct((B, T, GV), F32),
                   jax.ShapeDtypeStruct((B, H_G, DK_G, DV_G), F32)],
        scratch_shapes=[pltpu.VMEM((GQ, DV_G), F32)],
        compiler_params=_params(2),
        name="gla",
    )(r3(gq), r3(gk), r3(gv), r3(gr), r3(la), s0, gn)


def _lam(lp_ref, lam0):
    lp = lp_ref[...]
    t1 = jnp.sum(lp[0:1] * lp[1:2], axis=-1, keepdims=True)
    t2 = jnp.sum(lp[2:3] * lp[3:4], axis=-1, keepdims=True)
    return jnp.exp(t1) - jnp.exp(t2) + lam0


def _split_maps(q):
    lane = lax.broadcasted_iota(jnp.int32, q.shape, 1)
    return (jnp.where(lane < DH_D, q, 0.0).astype(BF16),
            jnp.where(lane >= DH_D, q, 0.0).astype(BF16))


def _attn_finish(acc0, l0, acc1, l1, lam, dn, lam0):
    o = acc0 * (1.0 / l0) - lam * (acc1 * (1.0 / l1))
    return _rms(o, dn) * (1.0 - lam0)


def _attn_prompt_kernel(lp_ref, q_ref, k_ref, vt_ref, bias_ref, dn_ref, o_ref,
                        qm_ref, s_ref, p_ref, alpha_ref, m_ref, l_ref, acc_ref, *, tile, lam0):
    i = pl.program_id(2)
    qt = (q_ref[0, 0] * LOG2E).T
    sub = lax.broadcasted_iota(jnp.int32, qt.shape, 0)
    qm_ref[0] = jnp.where(sub < DH_D, qt, 0.0).astype(BF16)
    qm_ref[1] = jnp.where(sub >= DH_D, qt, 0.0).astype(BF16)
    m_ref[...] = jnp.full(m_ref.shape, NEG_INIT, F32)
    l_ref[...] = jnp.zeros(l_ref.shape, F32)
    acc_ref[...] = jnp.zeros(acc_ref.shape, F32)

    def scores(j, slot):
        ks = pl.multiple_of(j * tile, tile)
        kt = k_ref[0, 0, pl.ds(ks, tile), :]
        for m in range(2):
            s_ref[slot, m] = _dot(kt, qm_ref[m])

    def weighted_values(j, slot):
        vt = vt_ref[0, 0, j]
        for m in range(2):
            acc_ref[m] = alpha_ref[slot, m] * acc_ref[m] + _dot(vt, p_ref[slot, m])

    def softmax_step(slot, near):
        for m in range(2):
            s = s_ref[slot, m]
            if near is not None:
                s = s + bias_ref[0, near]
            m_old = m_ref[m]
            m_new = jnp.maximum(m_old, jnp.max(s, axis=0, keepdims=True))
            p = jnp.exp2(s - m_new)
            p_ref[slot, m] = p.astype(BF16)
            alpha = jnp.exp2(m_old - m_new)
            alpha_ref[slot, m] = alpha
            l_ref[m] = alpha * l_ref[m] + jnp.sum(p, axis=0, keepdims=True)
            m_ref[m] = m_new

    def step(r, slot, *, near=None, values=True, next_scores=True):
        if next_scores:
            scores(jnp.maximum(i - r - 1, 0), 1 - slot)
        if values:
            weighted_values(i - r + 1, 1 - slot)
        softmax_step(slot, near)

    scores(i, 0)
    step(0, 0, near=0, values=False)

    @pl.when(i >= 1)
    def _():
        step(1, 1, near=1)

    def far_pair(c, carry):
        r = 2 * c + 2
        step(r, 0)
        step(r + 1, 1)
        return carry

    lax.fori_loop(0, jnp.maximum(i - 1, 0) // 2, far_pair, 0)

    @pl.when(jnp.logical_and(i >= 2, i % 2 == 0))
    def _():
        step(i, 0, next_scores=False)

    @pl.when(i % 2 == 0)
    def _():
        weighted_values(0, 0)

    @pl.when(i % 2 == 1)
    def _():
        weighted_values(0, 1)

    lam = _lam(lp_ref, lam0)
    ot = acc_ref[0] * (1.0 / l_ref[0]) - lam * (acc_ref[1] * (1.0 / l_ref[1]))
    o_ref[0] = _rms(ot.T, dn_ref[...]) * (1.0 - lam0)


def _attn_prompt(lp, dq, kb, vtb, bias, dn, lam0):
    B, H, T, _ = dq.shape
    tile = ATTN_TILE
    return pl.pallas_call(
        functools.partial(_attn_prompt_kernel, tile=tile, lam0=lam0),
        grid=(B, H, T // tile),
        in_specs=[_const_spec(lp.shape),
                  pl.BlockSpec((1, 1, tile, HD2), lambda b, h, i: (b, h, i, 0)),
                  pl.BlockSpec((1, 1, T, HD2), lambda b, h, i: (b, h, 0, 0)),
                  pl.BlockSpec((1, 1, T // tile, HD2, tile), lambda b, h, i: (b, h, 0, 0, 0)),
                  pl.BlockSpec((1, 2, tile, tile), lambda b, h, i: (h, 0, 0, 0)),
                  _const_spec((1, HD2))],
        out_specs=pl.BlockSpec((1, tile, HD2), lambda b, h, i: (b, i, h)),
        out_shape=jax.ShapeDtypeStruct((B, T, H * HD2), F32),
        scratch_shapes=[pltpu.VMEM((2, HD2, tile), BF16),
                        pltpu.VMEM((2, 2, tile, tile), F32),
                        pltpu.VMEM((2, 2, tile, tile), BF16),
                        pltpu.VMEM((2, 2, 1, tile), F32),
                        pltpu.VMEM((2, 1, tile), F32), pltpu.VMEM((2, 1, tile), F32),
                        pltpu.VMEM((2, HD2, tile), F32)],
        compiler_params=_params(3),
        name="attn_prompt",
    )(lp, dq, kb, vtb, bias, dn)


def _attn_step_kernel(lp_ref, q_ref, kp_ref, vp_ref, kn_ref, vn_ref, bias_ref, dn_ref, o_ref, *, past, lam0):
    qm = _split_maps(q_ref[0, 0])
    kp = kp_ref[0, 0].astype(BF16)
    vp = vp_ref[0, 0].astype(BF16)
    kn = kn_ref[0, 0].astype(BF16)
    vn = vn_ref[0, 0].astype(BF16)
    bias_p = bias_ref[0, 0, :, :past]
    bias_n = bias_ref[0, 0, :, past:]
    res = []
    for m in range(2):
        s_p = _dot_nt(qm[m], kp) + bias_p
        s_n = _dot_nt(qm[m], kn) + bias_n
        mx = jnp.maximum(jnp.max(s_p, axis=-1, keepdims=True), jnp.max(s_n, axis=-1, keepdims=True))
        p_p = jnp.exp(s_p - mx)
        p_n = jnp.exp(s_n - mx)
        l = jnp.sum(p_p, axis=-1, keepdims=True) + jnp.sum(p_n, axis=-1, keepdims=True)
        acc = _dot(p_p.astype(BF16), vp) + _dot(p_n.astype(BF16), vn)
        res += [acc, l]
    o_ref[0] = _attn_finish(res[0], res[1], res[2], res[3], _lam(lp_ref, lam0), dn_ref[...], lam0)


def _attn_step(lp, dq, k_past, v_past, dk, dv, bias, dn, lam0, layer):
    B, H, T, _ = dq.shape
    past = k_past.shape[3]
    qblk = pl.BlockSpec((1, 1, T, HD2), lambda b, h: (b, h, 0, 0))
    new = pl.BlockSpec((None, 1, 1, T, HD2), lambda b, h: (layer, b, h, 0, 0))
    old = pl.BlockSpec((None, 1, 1, past, HD2), lambda b, h: (layer, b, h, 0, 0))
    return pl.pallas_call(
        functools.partial(_attn_step_kernel, past=past, lam0=lam0),
        grid=(B, H),
        in_specs=[_const_spec(lp.shape), qblk, old, old, new, new,
                  pl.BlockSpec((1, 1, T, past + T), lambda b, h: (h, 0, 0, 0)),
                  _const_spec((1, HD2))],
        out_specs=pl.BlockSpec((1, T, HD2), lambda b, h: (b, 0, h)),
        out_shape=jax.ShapeDtypeStruct((B, T, H * HD2), F32),
        compiler_params=_params(2),
        name="attn_step",
    )(lp, dq, k_past, v_past, dk, dv, bias, dn)


SHIFT_PAD = 8


def _ffn_kernel(x_ref, og_ref, od_ref, wo_ref, lpost_ref, lpre_ref, lout_ref, wup_ref, cw_ref, cb_ref,
                cs_ref, wdn_ref, y_ref, cnew_ref, prev_ref, ext_ref, act_ref,
                *, n_seq, seq_rows, tiles_per_seq, d_ff, fc):
    ti = pl.program_id(0) % tiles_per_seq

    @pl.when(ti == 0)
    def _():
        prev_ref[...] = cs_ref[...]

    mix = _dot(og_ref[...].astype(BF16), wo_ref[:GV, :]) + _dot(od_ref[...].astype(BF16), wo_ref[GV:, :])
    x1 = x_ref[...] + _rms(mix, lpost_ref[...])
    h = _rms(x1, lpre_ref[...]).astype(BF16)
    n_chunks = d_ff // fc
    split = (n_chunks + 1) // 2 + 1
    k_gelu = math.sqrt(2.0 / math.pi)

    def up_pair(c):
        return [_dot(h, wup_ref[:, off:off + fc]) for off in (c * fc, d_ff + c * fc)]

    nxt = up_pair(0)
    for c in range(n_chunks):
        cur = nxt
        if c + 1 < n_chunks:
            nxt = up_pair(c + 1)
        halves = []
        for hf, off in enumerate((c * fc, d_ff + c * fc)):
            cols = slice(off, off + fc)
            buf = ext_ref.at[c % 2, hf]
            cw = cw_ref[:, cols]
            cb = cb_ref[:, cols]
            parts = []
            for s in range(n_seq):
                base = s * (seq_rows + SHIFT_PAD) + SHIFT_PAD
                buf[base - 2:base, :] = prev_ref[s, :, cols]
                buf[base:base + seq_rows, :] = cur[hf][s * seq_rows:(s + 1) * seq_rows, :]
                d1 = buf[base - 1:base - 1 + seq_rows, :]
                d2 = buf[base - 2:base - 2 + seq_rows, :]
                parts.append(cb + cw[0:1] * d2 + cw[1:2] * d1 + cw[2:3] * buf[base:base + seq_rows, :])
                prev_ref[s, :, cols] = buf[base + seq_rows - 2:base + seq_rows, :]
            halves.append(parts[0] if n_seq == 1 else jnp.concatenate(parts, axis=0))
        g, u = halves
        gelu = g * (0.5 + 0.5 * jnp.tanh(g * (k_gelu + (k_gelu * 0.044715) * (g * g))))
        act_ref[:, c * fc:(c + 1) * fc] = (gelu * u).astype(BF16)
        if c + 1 == split:
            down = _dot(act_ref[:, :split * fc], wdn_ref[:split * fc, :])
    down = down + _dot(act_ref[:, split * fc:], wdn_ref[split * fc:, :])
    y_ref[...] = x1 + _rms(down, lout_ref[...])
    cnew_ref[...] = prev_ref[...]


def _ffn(x2d, og2d, od2d, B, T, w_out, lpost, lpre, lout, w_up, conv_w, conv_b, conv_state, w_down):
    n_rows, D = x2d.shape
    f2 = w_up.shape[1]
    d_ff = f2 // 2
    if T >= ROW_TILE:
        tm, n_seq, seq_rows, tps = ROW_TILE, 1, ROW_TILE, T // ROW_TILE
    else:
        tm, n_seq, seq_rows, tps = n_rows, B, T, 1
    row = lambda w: pl.BlockSpec((tm, w), lambda i: (i, 0))
    cs = pl.BlockSpec((n_seq, CONV_W - 1, f2), lambda i: (i // tps, 0, 0))
    fc = 256
    return pl.pallas_call(
        functools.partial(_ffn_kernel, n_seq=n_seq, seq_rows=seq_rows, tiles_per_seq=tps, d_ff=d_ff, fc=fc),
        grid=(n_rows // tm,),
        in_specs=[row(D), row(GV), row(DA), _const_spec(w_out.shape), _const_spec((1, D)), _const_spec((1, D)),
                  _const_spec((1, D)), _const_spec(w_up.shape), _const_spec(conv_w.shape),
                  _const_spec((1, f2)), cs, _const_spec(w_down.shape)],
        out_specs=[row(D), cs],
        out_shape=[jax.ShapeDtypeStruct((n_rows, D), F32),
                   jax.ShapeDtypeStruct((B, CONV_W - 1, f2), F32)],
        scratch_shapes=[pltpu.VMEM((n_seq, CONV_W - 1, f2), F32),
                        pltpu.VMEM((2, 2, n_seq * (seq_rows + SHIFT_PAD), fc), F32),
                        pltpu.VMEM((tm, d_ff), BF16)],
        compiler_params=_params(1),
        name="ffn",
    )(x2d, og2d, od2d, w_out, lpost, lpre, lout, w_up, conv_w, conv_b, conv_state, w_down)


def _lambda_init(layer):
    return 0.8 - 0.6 * math.exp(-0.3 * layer)


def _layer(x2d, B, T, layer, depth, kv_slabs, past_k, past_v, gla_state, conv_state, bias, wts):
    (w_main, w_gd, w_gu, b_gu, gn, lp, dn, w_out, ln_pre, ln_post, lf_pre, lf_post,
     w_up, conv_w, conv_b, w_down) = wts
    prompt = past_k is None
    outs = _proj(x2d, B, T, ln_pre, w_main, w_gd, w_gu, b_gu, layer=layer, depth=depth, kv_slabs=kv_slabs,
                 emit_bf16=prompt)
    gq, gk, gv, gr, la, dq, dk, dv = outs[:8]
    og, s_new = _gla(gq, gk, gv, gr, la, gla_state, gn, B, T)
    lam0 = _lambda_init(layer)
    if prompt:
        od = _attn_prompt(lp, dq, outs[8], outs[9], bias, dn, lam0)
    else:
        od = _attn_step(lp, dq, past_k, past_v, dk, dv, bias, dn, lam0, layer)
    y, c_new = _ffn(x2d, og.reshape(B * T, GV), od.reshape(B * T, DA), B, T, w_out, ln_post, lf_pre, lf_post,
                    w_up, conv_w, conv_b, conv_state, w_down)
    return y, (dk, dv), s_new, c_new


def kernel(x_prompt, x_sample, cache_k, cache_v, state_gla, state_conv, t5_table, w_in, w_gate_up, b_gate_up,
           gla_norm, lam_params, diff_norm, w_out, ln_mix_pre, ln_mix_post, ln_ffn_pre, ln_ffn_post,
           w_ffn_up, conv_w, conv_b, w_ffn_down):
    depth = w_in.shape[0]
    B, T, D = x_prompt.shape
    Bs, Ts, _ = x_sample.shape
    past = cache_k.shape[3]
    f2 = w_ffn_up.shape[2]
    gd0 = 2 * GQ + 2 * GV

    def layer_weights(l):
        row = lambda a: a[l].reshape(1, -1)
        w_main = jnp.concatenate([w_in[l, :, :gd0], w_in[l, :, gd0 + GATE_RANK:]], axis=1).astype(BF16)
        return (w_main, w_in[l, :, gd0:gd0 + GATE_RANK].astype(BF16), w_gate_up[l].astype(BF16),
                row(b_gate_up), row(gla_norm), lam_params[l], row(diff_norm), w_out[l].astype(BF16),
                row(ln_mix_pre), row(ln_mix_post), row(ln_ffn_pre), row(ln_ffn_post),
                w_ffn_up[l].astype(BF16), conv_w[l], row(conv_b), w_ffn_down[l].astype(BF16))

    wts = [layer_weights(l) for l in range(depth)]
    bias_prompt = _bias_tiles(t5_table, [0, ATTN_TILE], ATTN_TILE, ATTN_TILE, keys_on_rows=True, exp2_shifted=True)
    bias_step = _bias_tiles(t5_table, [past], Ts, past + Ts)

    y = x_prompt.reshape(B * T, D)
    kv_p, gp, cp = None, [], []
    zero_s = jnp.zeros((B, H_G, DK_G, DV_G), F32)
    zero_c = jnp.zeros((B, CONV_W - 1, f2), F32)
    for l in range(depth):
        y, kv_p, s_new, c_new = _layer(y, B, T, l, depth, kv_p, None, None, zero_s, zero_c, bias_prompt, wts[l])
        gp.append(s_new); cp.append(c_new)
    y_prompt = y.reshape(B, T, D)

    y = x_sample.reshape(Bs * Ts, D)
    kv_s, gs, cs = None, [], []
    for l in range(depth):
        y, kv_s, s_new, c_new = _layer(y, Bs, Ts, l, depth, kv_s, cache_k, cache_v, state_gla[l],
                                       state_conv[l], bias_step, wts[l])
        gs.append(s_new); cs.append(c_new)
    y_sample = y.reshape(Bs, Ts, D)

    return (y_prompt, y_sample, kv_p[0], kv_p[1], jnp.stack(gp), jnp.stack(cp),
            kv_s[0], kv_s[1], jnp.stack(gs), jnp.stack(cs))
```

```python
import functools
import math

import jax
import jax.numpy as jnp
from jax import lax
from jax.experimental import pallas as pl
from jax.experimental.pallas import tpu as pltpu

F32 = jnp.float32
BF16 = jnp.bfloat16

CHUNK = 64
H_G = 4
DK_G = 64
DV_G = 128
GATE_RANK = 16
GATE_NORMALIZER = 16.0
H_D = 4
DH_D = 64
NUM_BUCKETS = 32
MAX_DISTANCE = 128
CONV_W = 3
EPS = 1e-6

GQ = H_G * DK_G
GV = H_G * DV_G
DA = H_D * 2 * DH_D
HD2 = 2 * DH_D

VMEM_LIMIT_BYTES = 56 * 1024 * 1024
ROW_TILE = 512
GLA_ROWS = 256
ATTN_TILE = 512
NEG_INIT = -1e30
LOG2E = math.log2(math.e)


def _rms(x, w):
    return x * lax.rsqrt(jnp.mean(x * x, axis=-1, keepdims=True) + EPS) * w


def _dot(a, b):
    return jnp.dot(a, b, preferred_element_type=F32)


def _dot_nt(a, b):
    return lax.dot_general(a, b, (((1,), (1,)), ((), ())), preferred_element_type=F32)


def _params(n_grid):
    return pltpu.CompilerParams(dimension_semantics=("arbitrary",) * n_grid,
                                vmem_limit_bytes=VMEM_LIMIT_BYTES)


def _const_spec(shape):
    nd = len(shape)
    return pl.BlockSpec(shape, lambda *_: (0,) * nd, pipeline_mode=pl.Buffered(1))


def _bias_kernel(tab_ref, o_ref, *, qstarts, nq, nk, keys_on_rows, exp2_shifted):
    h = pl.program_id(0)
    nb = NUM_BUCKETS // 2
    max_exact = nb // 2
    shape = (nk, nq) if keys_on_rows else (nq, nk)
    q_axis, k_axis = (1, 0) if keys_on_rows else (0, 1)
    for d, q0 in enumerate(qstarts):
        qpos = q0 + lax.broadcasted_iota(jnp.int32, shape, q_axis)
        kpos = lax.broadcasted_iota(jnp.int32, shape, k_axis)
        rel = kpos - qpos
        ret = jnp.where(rel > 0, nb, 0)
        n = jnp.abs(rel)
        nf = jnp.maximum(n, 1).astype(F32)
        large = max_exact + (jnp.log(nf / max_exact) / math.log(MAX_DISTANCE / max_exact)
                             * (nb - max_exact)).astype(jnp.int32)
        large = jnp.minimum(large, nb - 1)
        bucket = ret + jnp.where(n < max_exact, n, large)
        bias = jnp.zeros(shape, F32)
        for b in range(NUM_BUCKETS):
            bias = jnp.where(bucket == b, tab_ref[b, h], bias)
        if exp2_shifted:
            bias = (bias - tab_ref[nb - 1, h]) * LOG2E
        visible = (kpos // CHUNK) <= (qpos // CHUNK)
        o_ref[0, d] = jnp.where(visible, bias, -jnp.inf)


def _bias_tiles(t5_table, qstarts, nq, nk, *, keys_on_rows=False, exp2_shifted=False):
    nd = len(qstarts)
    shape = (nk, nq) if keys_on_rows else (nq, nk)
    return pl.pallas_call(
        functools.partial(_bias_kernel, qstarts=tuple(qstarts), nq=nq, nk=nk, keys_on_rows=keys_on_rows,
                          exp2_shifted=exp2_shifted),
        grid=(H_D,),
        in_specs=[pl.BlockSpec(memory_space=pltpu.SMEM)],
        out_specs=pl.BlockSpec((1, nd) + shape, lambda h: (h, 0, 0, 0)),
        out_shape=jax.ShapeDtypeStruct((H_D, nd) + shape, F32),
        compiler_params=_params(1),
        name="t5_bias_tiles",
    )(t5_table)


def _proj_kernel(*refs, n_seq, seq_rows, n_alias, prompt):
    x_ref, ln_ref, w_ref, wgd_ref, wgu_ref, bgu_ref = refs[:6]
    gq_ref, gk_ref, gv_ref, gr_ref, la_ref, dk_ref, dv_ref = refs[6 + n_alias:13 + n_alias]
    if prompt:
        qt_ref, kb_ref, vt_ref = refs[13 + n_alias:]
    else:
        dq_ref, = refs[13 + n_alias:]
    h = _rms(x_ref[...], ln_ref[...]).astype(BF16)

    def seg(a, b):
        return _dot(h, w_ref[:, a:b])

    gq_ref[...] = seg(0, GQ) * DK_G ** -0.5
    gk_ref[...] = seg(GQ, 2 * GQ)
    gv_ref[...] = seg(2 * GQ, 2 * GQ + GV)
    gr_ref[...] = seg(2 * GQ + GV, 2 * GQ + 2 * GV)
    gd = _dot(h, wgd_ref[...]).astype(BF16)
    z = _dot(gd, wgu_ref[...]) + bgu_ref[...]
    log_sig = jnp.minimum(z, 0.0) - jnp.log1p(jnp.exp(-jnp.abs(z)))
    la_ref[...] = log_sig / GATE_NORMALIZER

    base = 2 * GQ + 2 * GV
    q = seg(base, base + DA) * DH_D ** -0.5
    k = seg(base + DA, base + 2 * DA)
    v = seg(base + 2 * DA, base + 3 * DA)
    feat = lax.broadcasted_iota(jnp.int32, (HD2, seq_rows), 0)
    for s in range(n_seq):
        rows = slice(s * seq_rows, (s + 1) * seq_rows)
        for hh in range(H_D):
            cols = slice(hh * HD2, (hh + 1) * HD2)
            dk_ref[s, hh] = k[rows, cols]
            dv_ref[s, hh] = v[rows, cols]
            if not prompt:
                dq_ref[s, hh] = q[rows, cols]
                continue
            qt = (q[rows, cols] * LOG2E).T
            qt_ref[s, hh, 0] = jnp.where(feat < DH_D, qt, 0.0).astype(BF16)
            qt_ref[s, hh, 1] = jnp.where(feat >= DH_D, qt, 0.0).astype(BF16)
            kb_ref[s, hh] = k[rows, cols].astype(BF16)
            for kb in range(seq_rows // ATTN_TILE):
                r0 = s * seq_rows + kb * ATTN_TILE
                vt_ref[s, hh, kb] = v[r0:r0 + ATTN_TILE, cols].T.astype(BF16)


def _proj(x2d, B, T, ln, w_main, w_gd, w_gu, b_gu, *, layer, depth, kv_slabs, emit_bf16):
    n_rows, D = x2d.shape
    if T >= ROW_TILE:
        tm, n_seq, seq_rows, tps = ROW_TILE, 1, ROW_TILE, T // ROW_TILE
    else:
        tm, n_seq, seq_rows, tps = n_rows, B, T, 1
    grid = (n_rows // tm,)
    row = lambda w: pl.BlockSpec((tm, w), lambda i: (i, 0))
    head = pl.BlockSpec((n_seq, H_D, seq_rows, HD2), lambda i: (i // tps, 0, i % tps, 0))
    slab = pl.BlockSpec((None, n_seq, H_D, seq_rows, HD2), lambda i: (layer, i // tps, 0, i % tps, 0))
    f32_rows = lambda w: jax.ShapeDtypeStruct((n_rows, w), F32)
    head_shape = lambda dt: jax.ShapeDtypeStruct((B, H_D, T, HD2), dt)
    slab_shape = jax.ShapeDtypeStruct((depth, B, H_D, T, HD2), F32)
    out_specs = [row(GQ), row(GQ), row(GV), row(GV), row(GQ), slab, slab]
    out_shape = [f32_rows(GQ), f32_rows(GQ), f32_rows(GV), f32_rows(GV), f32_rows(GQ), slab_shape, slab_shape]
    if emit_bf16:
        kt = seq_rows // ATTN_TILE
        out_specs += [pl.BlockSpec((n_seq, H_D, 2, HD2, seq_rows), lambda i: (i // tps, 0, 0, 0, i % tps)),
                      head,
                      pl.BlockSpec((n_seq, H_D, kt, HD2, ATTN_TILE), lambda i: (i // tps, 0, i % tps, 0, 0))]
        out_shape += [jax.ShapeDtypeStruct((B, H_D, 2, HD2, T), BF16), head_shape(BF16),
                      jax.ShapeDtypeStruct((B, H_D, T // ATTN_TILE, HD2, ATTN_TILE), BF16)]
    else:
        out_specs += [head]
        out_shape += [head_shape(F32)]
    args = [x2d, ln, w_main, w_gd, w_gu, b_gu]
    in_specs = [row(D), _const_spec((1, D)), _const_spec(w_main.shape), _const_spec(w_gd.shape),
                _const_spec(w_gu.shape), _const_spec((1, GQ))]
    aliases = {}
    if kv_slabs is not None:
        aliases = {len(args): 5, len(args) + 1: 6}
        args += list(kv_slabs)
        in_specs += [pl.BlockSpec(memory_space=pl.ANY)] * 2
    return pl.pallas_call(
        functools.partial(_proj_kernel, n_seq=n_seq, seq_rows=seq_rows, n_alias=len(aliases), prompt=emit_bf16),
        grid=grid,
        in_specs=in_specs,
        out_specs=out_specs,
        out_shape=out_shape,
        input_output_aliases=aliases,
        compiler_params=_params(1),
        name="proj",
    )(*args)


def _gla_kernel(q_ref, k_ref, v_ref, r_ref, la_ref, s0_ref, gn_ref, o_ref, sout_ref, s_ref, *, n_chunks):
    t = pl.program_id(1)

    @pl.when(t == 0)
    def _():
        for hh in range(H_G):
            s_ref[hh * DK_G:(hh + 1) * DK_G, :] = s0_ref[0, hh]

    R = n_chunks * CHUNK
    ri = lax.broadcasted_iota(jnp.int32, (R, R), 0)
    ci = lax.broadcasted_iota(jnp.int32, (R, R), 1)
    causal = jnp.logical_and((ri // CHUNK) == (ci // CHUNK), ci <= ri)
    head_of_lane = lax.broadcasted_iota(jnp.int32, (R, GQ), 1) // DK_G

    la = la_ref[0]
    a1 = la.astype(BF16)
    r1 = la - a1.astype(F32)
    a2 = r1.astype(BF16)
    a3 = (r1 - a2.astype(F32)).astype(BF16)
    tril = causal.astype(BF16)
    g = _dot(tril, a1) + _dot(tril, a2) + _dot(tril, a3)
    chunk_rows = [slice(c * CHUNK, (c + 1) * CHUNK) for c in range(n_chunks)]
    last_rows = [g[(c + 1) * CHUNK - 1:(c + 1) * CHUNK, :] for c in range(n_chunks)]
    g_last = jnp.concatenate([jnp.broadcast_to(x, (CHUNK, GQ)) for x in last_rows], axis=0)
    pad = [jnp.zeros((8 - n_chunks % 8, GQ), F32)] if n_chunks % 8 else []
    decay = jnp.exp(jnp.concatenate(last_rows + pad, axis=0).T)
    q = q_ref[0]
    k = k_ref[0]
    v = v_ref[0].astype(BF16)
    q_dec = q * jnp.exp(g)
    k_inv = (k * jnp.exp(-g)).astype(BF16)
    k_end = k * jnp.exp(g_last - g)
    qm = [jnp.where(head_of_lane == hh, q_dec, 0.0).astype(BF16) for hh in range(H_G)]
    vcols = [slice(hh * DV_G, (hh + 1) * DV_G) for hh in range(H_G)]

    o_inter = [[] for _ in range(H_G)]
    state = s_ref[...]
    for c, rows in enumerate(chunk_rows):
        s_b = state.astype(BF16)
        kt = k_end[rows].T.astype(BF16)
        upd = []
        for hh in range(H_G):
            o_inter[hh].append(_dot(qm[hh][rows], s_b))
            upd.append(_dot(kt[hh * DK_G:(hh + 1) * DK_G, :], v[rows, vcols[hh]]))
        state = decay[:, c:c + 1] * state + jnp.concatenate(upd, axis=0)
    s_ref[...] = state

    gn = gn_ref[...]
    for hh in range(H_G):
        a = jnp.where(causal, _dot_nt(qm[hh], k_inv), 0.0)
        o = _dot(a.astype(BF16), v[:, vcols[hh]]) + jnp.concatenate(o_inter[hh], axis=0)
        r = r_ref[0, :, vcols[hh]]
        o_ref[0, :, vcols[hh]] = _rms(o, gn) * (r * (1.0 / (1.0 + jnp.exp(-r))))

    for hh in range(H_G):
        sout_ref[0, hh] = s_ref[hh * DK_G:(hh + 1) * DK_G, :]


def _gla(gq, gk, gv, gr, la, s0, gn, B, T):
    tb = min(GLA_ROWS, T)
    r3 = lambda a: a.reshape(B, T, a.shape[-1])
    blk = lambda w: pl.BlockSpec((1, tb, w), lambda b, t: (b, t, 0))
    st = pl.BlockSpec((1, H_G, DK_G, DV_G), lambda b, t: (b, 0, 0, 0))
    return pl.pallas_call(
        functools.partial(_gla_kernel, n_chunks=tb // CHUNK),
        grid=(B, T // tb),
        in_specs=[blk(GQ), blk(GQ), blk(GV), blk(GV), blk(GQ), st, _const_spec((1, DV_G))],
        out_specs=[blk(GV), st],
        out_shape=[jax.ShapeDtypeStruct((B, T, GV), F32),
                   jax.ShapeDtypeStruct((B, H_G, DK_G, DV_G), F32)],
        scratch_shapes=[pltpu.VMEM((GQ, DV_G), F32)],
        compiler_params=_params(2),
        name="gla",
    )(r3(gq), r3(gk), r3(gv), r3(gr), r3(la), s0, gn)


def _lam(lp_ref, lam0):
    lp = lp_ref[...]
    t1 = jnp.sum(lp[0:1] * lp[1:2], axis=-1, keepdims=True)
    t2 = jnp.sum(lp[2:3] * lp[3:4], axis=-1, keepdims=True)
    return jnp.exp(t1) - jnp.exp(t2) + lam0


def _split_maps(q):
    lane = lax.broadcasted_iota(jnp.int32, q.shape, 1)
    return (jnp.where(lane < DH_D, q, 0.0).astype(BF16),
            jnp.where(lane >= DH_D, q, 0.0).astype(BF16))


def _attn_finish(acc0, l0, acc1, l1, lam, dn, lam0):
    o = acc0 * (1.0 / l0) - lam * (acc1 * (1.0 / l1))
    return _rms(o, dn) * (1.0 - lam0)


def _attn_prompt_kernel(lp_ref, q_ref, k_ref, vt_ref, bias_ref, dn_ref, o_ref,
                        s_ref, p_ref, alpha_ref, m_ref, l_ref, acc_ref, *, tile, lam0):
    i = pl.program_id(2)
    m_ref[...] = jnp.full(m_ref.shape, NEG_INIT, F32)
    l_ref[...] = jnp.zeros(l_ref.shape, F32)
    acc_ref[...] = jnp.zeros(acc_ref.shape, F32)
    both = (0, 1)

    def scores(j, slot, maps=both):
        ks = pl.multiple_of(j * tile, tile)
        kt = k_ref[0, 0, pl.ds(ks, tile), :]
        for m in maps:
            s_ref[slot, m] = _dot(kt, q_ref[0, 0, m])

    def weighted_values(j, slot, maps=both):
        vt = vt_ref[0, 0, j]
        for m in maps:
            acc_ref[m] = alpha_ref[slot, m] * acc_ref[m] + _dot(vt, p_ref[slot, m])

    def softmax_map(slot, near, m):
        s = s_ref[slot, m]
        if near is not None:
            s = s + bias_ref[0, near]
        m_old = m_ref[m]
        m_new = jnp.maximum(m_old, jnp.max(s, axis=0, keepdims=True))
        p = jnp.exp2(s - m_new)
        p_ref[slot, m] = p.astype(BF16)
        alpha = jnp.exp2(m_old - m_new)
        alpha_ref[slot, m] = alpha
        l_ref[m] = alpha * l_ref[m] + jnp.sum(p, axis=0, keepdims=True)
        m_ref[m] = m_new

    def step(r, slot, *, near=None, first=False, last=False):
        if first:
            scores(i, slot, maps=(1,))
        if not last:
            scores(jnp.maximum(i - r - 1, 0), 1 - slot)
        if not first:
            weighted_values(i - r + 1, 1 - slot)
        softmax_map(slot, near, 0)
        if last:
            weighted_values(i - r, slot, maps=(0,))
        softmax_map(slot, near, 1)
        if last:
            weighted_values(i - r, slot, maps=(1,))

    scores(i, 0, maps=(0,))
    for last in (False, True):
        pl.when((i == 0) == last)(functools.partial(step, 0, 0, near=0, first=True, last=last))
    for last in (False, True):
        pl.when(jnp.logical_and(i >= 1, (i == 1) == last))(functools.partial(step, 1, 1, near=1, last=last))

    def far_pair(c, carry):
        r = 2 * c + 2
        step(r, 0)
        step(r + 1, 1)
        return carry

    n_mid = jnp.maximum(i - 2, 0)
    lax.fori_loop(0, n_mid // 2, far_pair, 0)
    pl.when(jnp.logical_and(i >= 3, i % 2 == 1))(functools.partial(step, i - 1, 0))
    pl.when(jnp.logical_and(i >= 2, i % 2 == 0))(functools.partial(step, i, 0, last=True))
    pl.when(jnp.logical_and(i >= 3, i % 2 == 1))(functools.partial(step, i, 1, last=True))

    lam = _lam(lp_ref, lam0)
    ot = acc_ref[0] * (1.0 / l_ref[0]) - lam * (acc_ref[1] * (1.0 / l_ref[1]))
    scale = lax.rsqrt(jnp.mean(ot * ot, axis=0, keepdims=True) + EPS) * (1.0 - lam0)
    o_ref[0, 0] = ot * scale * dn_ref[...]


def _attn_prompt(lp, qt, kb, vtb, bias, dn_col, lam0):
    B, H, _, _, T = qt.shape
    tile = ATTN_TILE
    return pl.pallas_call(
        functools.partial(_attn_prompt_kernel, tile=tile, lam0=lam0),
        grid=(B, H, T // tile),
        in_specs=[_const_spec(lp.shape),
                  pl.BlockSpec((1, 1, 2, HD2, tile), lambda b, h, i: (b, h, 0, 0, i)),
                  pl.BlockSpec((1, 1, T, HD2), lambda b, h, i: (b, h, 0, 0)),
                  pl.BlockSpec((1, 1, T // tile, HD2, tile), lambda b, h, i: (b, h, 0, 0, 0)),
                  pl.BlockSpec((1, 2, tile, tile), lambda b, h, i: (h, 0, 0, 0)),
                  _const_spec((HD2, 1))],
        out_specs=pl.BlockSpec((1, 1, HD2, tile), lambda b, h, i: (b, h, 0, i)),
        out_shape=jax.ShapeDtypeStruct((B, H, HD2, T), F32),
        scratch_shapes=[pltpu.VMEM((2, 2, tile, tile), F32),
                        pltpu.VMEM((2, 2, tile, tile), BF16),
                        pltpu.VMEM((2, 2, 1, tile), F32),
                        pltpu.VMEM((2, 1, tile), F32), pltpu.VMEM((2, 1, tile), F32),
                        pltpu.VMEM((2, HD2, tile), F32)],
        compiler_params=_params(3),
        name="attn_prompt",
    )(lp, qt, kb, vtb, bias, dn_col)


def _attn_step_kernel(lp_ref, q_ref, kp_ref, vp_ref, kn_ref, vn_ref, bias_ref, dn_ref, o_ref, *, past, lam0):
    qm = _split_maps(q_ref[0, 0])
    kp = kp_ref[0, 0].astype(BF16)
    vp = vp_ref[0, 0].astype(BF16)
    kn = kn_ref[0, 0].astype(BF16)
    vn = vn_ref[0, 0].astype(BF16)
    bias_p = bias_ref[0, 0, :, :past]
    bias_n = bias_ref[0, 0, :, past:]
    res = []
    for m in range(2):
        s_p = _dot_nt(qm[m], kp) + bias_p
        s_n = _dot_nt(qm[m], kn) + bias_n
        mx = jnp.maximum(jnp.max(s_p, axis=-1, keepdims=True), jnp.max(s_n, axis=-1, keepdims=True))
        p_p = jnp.exp(s_p - mx)
        p_n = jnp.exp(s_n - mx)
        l = jnp.sum(p_p, axis=-1, keepdims=True) + jnp.sum(p_n, axis=-1, keepdims=True)
        acc = _dot(p_p.astype(BF16), vp) + _dot(p_n.astype(BF16), vn)
        res += [acc, l]
    o_ref[0] = _attn_finish(res[0], res[1], res[2], res[3], _lam(lp_ref, lam0), dn_ref[...], lam0)


def _attn_step(lp, dq, k_past, v_past, dk, dv, bias, dn, lam0, layer):
    B, H, T, _ = dq.shape
    past = k_past.shape[3]
    qblk = pl.BlockSpec((1, 1, T, HD2), lambda b, h: (b, h, 0, 0))
    new = pl.BlockSpec((None, 1, 1, T, HD2), lambda b, h: (layer, b, h, 0, 0))
    old = pl.BlockSpec((None, 1, 1, past, HD2), lambda b, h: (layer, b, h, 0, 0))
    return pl.pallas_call(
        functools.partial(_attn_step_kernel, past=past, lam0=lam0),
        grid=(B, H),
        in_specs=[_const_spec(lp.shape), qblk, old, old, new, new,
                  pl.BlockSpec((1, 1, T, past + T), lambda b, h: (h, 0, 0, 0)),
                  _const_spec((1, HD2))],
        out_specs=pl.BlockSpec((1, T, HD2), lambda b, h: (b, 0, h)),
        out_shape=jax.ShapeDtypeStruct((B, T, H * HD2), F32),
        compiler_params=_params(2),
        name="attn_step",
    )(lp, dq, k_past, v_past, dk, dv, bias, dn)


SHIFT_PAD = 8


def _ffn_kernel(x_ref, og_ref, od_ref, wo_ref, lpost_ref, lpre_ref, lout_ref, wup_ref, cw_ref, cb_ref,
                cs_ref, wdn_ref, y_ref, cnew_ref, prev_ref, ext_ref, act_ref,
                *, n_seq, seq_rows, tiles_per_seq, d_ff, fc, od_transposed):
    ti = pl.program_id(0) % tiles_per_seq

    @pl.when(ti == 0)
    def _():
        prev_ref[...] = cs_ref[...]

    if od_transposed:
        od = jnp.concatenate([od_ref[0, hh].T for hh in range(H_D)], axis=1)
    else:
        od = od_ref[...]
    mix = _dot(og_ref[...].astype(BF16), wo_ref[:GV, :]) + _dot(od.astype(BF16), wo_ref[GV:, :])
    x1 = x_ref[...] + _rms(mix, lpost_ref[...])
    h = _rms(x1, lpre_ref[...]).astype(BF16)
    n_chunks = d_ff // fc
    split = (n_chunks + 1) // 2 + 1
    k_gelu = math.sqrt(2.0 / math.pi)

    def up_pair(c):
        return [_dot(h, wup_ref[:, off:off + fc]) for off in (c * fc, d_ff + c * fc)]

    nxt = up_pair(0)
    for c in range(n_chunks):
        cur = nxt
        if c + 1 < n_chunks:
            nxt = up_pair(c + 1)
        halves = []
        for hf, off in enumerate((c * fc, d_ff + c * fc)):
            cols = slice(off, off + fc)
            buf = ext_ref.at[c % 2, hf]
            cw = cw_ref[:, cols]
            cb = cb_ref[:, cols]
            parts = []
            for s in range(n_seq):
                base = s * (seq_rows + SHIFT_PAD) + SHIFT_PAD
                buf[base - 2:base, :] = prev_ref[s, :, cols]
                buf[base:base + seq_rows, :] = cur[hf][s * seq_rows:(s + 1) * seq_rows, :]
                d1 = buf[base - 1:base - 1 + seq_rows, :]
                d2 = buf[base - 2:base - 2 + seq_rows, :]
                parts.append(cb + cw[0:1] * d2 + cw[1:2] * d1 + cw[2:3] * buf[base:base + seq_rows, :])
                prev_ref[s, :, cols] = buf[base + seq_rows - 2:base + seq_rows, :]
            halves.append(parts[0] if n_seq == 1 else jnp.concatenate(parts, axis=0))
        g, u = halves
        gelu = g * (0.5 + 0.5 * jnp.tanh(g * (k_gelu + (k_gelu * 0.044715) * (g * g))))
        act_ref[:, c * fc:(c + 1) * fc] = (gelu * u).astype(BF16)
        if c + 1 == split:
            down = _dot(act_ref[:, :split * fc], wdn_ref[:split * fc, :])
    down = down + _dot(act_ref[:, split * fc:], wdn_ref[split * fc:, :])
    y_ref[...] = x1 + _rms(down, lout_ref[...])
    cnew_ref[...] = prev_ref[...]


def _ffn(x2d, og2d, od, B, T, w_out, lpost, lpre, lout, w_up, conv_w, conv_b, conv_state, w_down):
    n_rows, D = x2d.shape
    f2 = w_up.shape[1]
    d_ff = f2 // 2
    if T >= ROW_TILE:
        tm, n_seq, seq_rows, tps = ROW_TILE, 1, ROW_TILE, T // ROW_TILE
    else:
        tm, n_seq, seq_rows, tps = n_rows, B, T, 1
    row = lambda w: pl.BlockSpec((tm, w), lambda i: (i, 0))
    cs = pl.BlockSpec((n_seq, CONV_W - 1, f2), lambda i: (i // tps, 0, 0))
    od_transposed = od.ndim == 4
    if od_transposed:
        assert n_seq == 1
        od_spec = pl.BlockSpec((1, H_D, HD2, tm), lambda i: (i // tps, 0, 0, i % tps))
    else:
        od_spec = row(DA)
    fc = 256
    return pl.pallas_call(
        functools.partial(_ffn_kernel, n_seq=n_seq, seq_rows=seq_rows, tiles_per_seq=tps, d_ff=d_ff, fc=fc,
                          od_transposed=od_transposed),
        grid=(n_rows // tm,),
        in_specs=[row(D), row(GV), od_spec, _const_spec(w_out.shape), _const_spec((1, D)), _const_spec((1, D)),
                  _const_spec((1, D)), _const_spec(w_up.shape), _const_spec(conv_w.shape),
                  _const_spec((1, f2)), cs, _const_spec(w_down.shape)],
        out_specs=[row(D), cs],
        out_shape=[jax.ShapeDtypeStruct((n_rows, D), F32),
                   jax.ShapeDtypeStruct((B, CONV_W - 1, f2), F32)],
        scratch_shapes=[pltpu.VMEM((n_seq, CONV_W - 1, f2), F32),
                        pltpu.VMEM((2, 2, n_seq * (seq_rows + SHIFT_PAD), fc), F32),
                        pltpu.VMEM((tm, d_ff), BF16)],
        compiler_params=_params(1),
        name="ffn",
    )(x2d, og2d, od, w_out, lpost, lpre, lout, w_up, conv_w, conv_b, conv_state, w_down)


def _lambda_init(layer):
    return 0.8 - 0.6 * math.exp(-0.3 * layer)


def _layer(x2d, B, T, layer, depth, kv_slabs, past_k, past_v, gla_state, conv_state, bias, wts):
    (w_main, w_gd, w_gu, b_gu, gn, lp, dn, w_out, ln_pre, ln_post, lf_pre, lf_post,
     w_up, conv_w, conv_b, w_down) = wts
    prompt = past_k is None
    outs = _proj(x2d, B, T, ln_pre, w_main, w_gd, w_gu, b_gu, layer=layer, depth=depth, kv_slabs=kv_slabs,
                 emit_bf16=prompt)
    gq, gk, gv, gr, la, dk, dv = outs[:7]
    og, s_new = _gla(gq, gk, gv, gr, la, gla_state, gn, B, T)
    lam0 = _lambda_init(layer)
    if prompt:
        qt, kb, vtb = outs[7:]
        od = _attn_prompt(lp, qt, kb, vtb, bias, dn.reshape(-1, 1), lam0)
    else:
        od = _attn_step(lp, outs[7], past_k, past_v, dk, dv, bias, dn, lam0, layer).reshape(B * T, DA)
    y, c_new = _ffn(x2d, og.reshape(B * T, GV), od, B, T, w_out, ln_post, lf_pre, lf_post,
                    w_up, conv_w, conv_b, conv_state, w_down)
    return y, (dk, dv), s_new, c_new


def kernel(x_prompt, x_sample, cache_k, cache_v, state_gla, state_conv, t5_table, w_in, w_gate_up, b_gate_up,
           gla_norm, lam_params, diff_norm, w_out, ln_mix_pre, ln_mix_post, ln_ffn_pre, ln_ffn_post,
           w_ffn_up, conv_w, conv_b, w_ffn_down):
    depth = w_in.shape[0]
    B, T, D = x_prompt.shape
    Bs, Ts, _ = x_sample.shape
    past = cache_k.shape[3]
    f2 = w_ffn_up.shape[2]
    gd0 = 2 * GQ + 2 * GV

    def layer_weights(l):
        row = lambda a: a[l].reshape(1, -1)
        w_main = jnp.concatenate([w_in[l, :, :gd0], w_in[l, :, gd0 + GATE_RANK:]], axis=1).astype(BF16)
        return (w_main, w_in[l, :, gd0:gd0 + GATE_RANK].astype(BF16), w_gate_up[l].astype(BF16),
                row(b_gate_up), row(gla_norm), lam_params[l], row(diff_norm), w_out[l].astype(BF16),
                row(ln_mix_pre), row(ln_mix_post), row(ln_ffn_pre), row(ln_ffn_post),
                w_ffn_up[l].astype(BF16), conv_w[l], row(conv_b), w_ffn_down[l].astype(BF16))

    wts = [layer_weights(l) for l in range(depth)]
    bias_prompt = _bias_tiles(t5_table, [0, ATTN_TILE], ATTN_TILE, ATTN_TILE, keys_on_rows=True, exp2_shifted=True)
    bias_step = _bias_tiles(t5_table, [past], Ts, past + Ts)

    y = x_prompt.reshape(B * T, D)
    kv_p, gp, cp = None, [], []
    zero_s = jnp.zeros((B, H_G, DK_G, DV_G), F32)
    zero_c = jnp.zeros((B, CONV_W - 1, f2), F32)
    for l in range(depth):
        y, kv_p, s_new, c_new = _layer(y, B, T, l, depth, kv_p, None, None, zero_s, zero_c, bias_prompt, wts[l])
        gp.append(s_new); cp.append(c_new)
    y_prompt = y.reshape(B, T, D)

    y = x_sample.reshape(Bs * Ts, D)
    kv_s, gs, cs = None, [], []
    for l in range(depth):
        y, kv_s, s_new, c_new = _layer(y, Bs, Ts, l, depth, kv_s, cache_k, cache_v, state_gla[l],
                                       state_conv[l], bias_step, wts[l])
        gs.append(s_new); cs.append(c_new)
    y_sample = y.reshape(Bs, Ts, D)

    return (y_prompt, y_sample, kv_p[0], kv_p[1], jnp.stack(gp), jnp.stack(cp),
            kv_s[0], kv_s[1], jnp.stack(gs), jnp.stack(cs))
```

```python
import functools
import math

import jax
import jax.numpy as jnp
from jax import lax
from jax.experimental import pallas as pl
from jax.experimental.pallas import tpu as pltpu

F32 = jnp.float32
BF16 = jnp.bfloat16

CHUNK = 64
H_G = 4
DK_G = 64
DV_G = 128
GATE_RANK = 16
GATE_NORMALIZER = 16.0
H_D = 4
DH_D = 64
NUM_BUCKETS = 32
MAX_DISTANCE = 128
CONV_W = 3
EPS = 1e-6

GQ = H_G * DK_G
GV = H_G * DV_G
DA = H_D * 2 * DH_D
HD2 = 2 * DH_D

VMEM_LIMIT_BYTES = 56 * 1024 * 1024
ROW_TILE = 512
GLA_ROWS = 256
ATTN_TILE = 512
NEG_INIT = -1e30
LOG2E = math.log2(math.e)


def _rms(x, w):
    return x * lax.rsqrt(jnp.mean(x * x, axis=-1, keepdims=True) + EPS) * w


def _dot(a, b):
    return jnp.dot(a, b, preferred_element_type=F32)


def _dot_nt(a, b):
    return lax.dot_general(a, b, (((1,), (1,)), ((), ())), preferred_element_type=F32)


def _params(n_grid):
    return pltpu.CompilerParams(dimension_semantics=("arbitrary",) * n_grid,
                                vmem_limit_bytes=VMEM_LIMIT_BYTES)


def _const_spec(shape):
    nd = len(shape)
    return pl.BlockSpec(shape, lambda *_: (0,) * nd, pipeline_mode=pl.Buffered(1))


def _bias_kernel(tab_ref, o_ref, *, qstarts, nq, nk, keys_on_rows, exp2_shifted):
    h = pl.program_id(0)
    nb = NUM_BUCKETS // 2
    max_exact = nb // 2
    shape = (nk, nq) if keys_on_rows else (nq, nk)
    q_axis, k_axis = (1, 0) if keys_on_rows else (0, 1)
    for d, q0 in enumerate(qstarts):
        qpos = q0 + lax.broadcasted_iota(jnp.int32, shape, q_axis)
        kpos = lax.broadcasted_iota(jnp.int32, shape, k_axis)
        rel = kpos - qpos
        ret = jnp.where(rel > 0, nb, 0)
        n = jnp.abs(rel)
        nf = jnp.maximum(n, 1).astype(F32)
        large = max_exact + (jnp.log(nf / max_exact) / math.log(MAX_DISTANCE / max_exact)
                             * (nb - max_exact)).astype(jnp.int32)
        large = jnp.minimum(large, nb - 1)
        bucket = ret + jnp.where(n < max_exact, n, large)
        bias = jnp.zeros(shape, F32)
        for b in range(NUM_BUCKETS):
            bias = jnp.where(bucket == b, tab_ref[b, h], bias)
        if exp2_shifted:
            bias = (bias - tab_ref[nb - 1, h]) * LOG2E
        visible = (kpos // CHUNK) <= (qpos // CHUNK)
        o_ref[0, d] = jnp.where(visible, bias, -jnp.inf)


def _bias_tiles(t5_table, qstarts, nq, nk, *, keys_on_rows=False, exp2_shifted=False):
    nd = len(qstarts)
    shape = (nk, nq) if keys_on_rows else (nq, nk)
    return pl.pallas_call(
        functools.partial(_bias_kernel, qstarts=tuple(qstarts), nq=nq, nk=nk, keys_on_rows=keys_on_rows,
                          exp2_shifted=exp2_shifted),
        grid=(H_D,),
        in_specs=[pl.BlockSpec(memory_space=pltpu.SMEM)],
        out_specs=pl.BlockSpec((1, nd) + shape, lambda h: (h, 0, 0, 0)),
        out_shape=jax.ShapeDtypeStruct((H_D, nd) + shape, F32),
        compiler_params=_params(1),
        name="t5_bias_tiles",
    )(t5_table)


def _proj_kernel(*refs, n_seq, seq_rows, n_alias, prompt):
    x_ref, ln_ref, w_ref, wgd_ref, wgu_ref, bgu_ref = refs[:6]
    gq_ref, gk_ref, gv_ref, gr_ref, la_ref, dk_ref, dv_ref = refs[6 + n_alias:13 + n_alias]
    if prompt:
        qt_ref, kb_ref, vt_ref = refs[13 + n_alias:]
    else:
        dq_ref, = refs[13 + n_alias:]
    h = _rms(x_ref[...], ln_ref[...]).astype(BF16)

    def seg(a, b):
        return _dot(h, w_ref[:, a:b])

    gq_ref[...] = seg(0, GQ) * DK_G ** -0.5
    gk_ref[...] = seg(GQ, 2 * GQ)
    gv_ref[...] = seg(2 * GQ, 2 * GQ + GV)
    gr_ref[...] = seg(2 * GQ + GV, 2 * GQ + 2 * GV)
    gd = _dot(h, wgd_ref[...]).astype(BF16)
    z = _dot(gd, wgu_ref[...]) + bgu_ref[...]
    log_sig = jnp.minimum(z, 0.0) - jnp.log1p(jnp.exp(-jnp.abs(z)))
    la_ref[...] = log_sig / GATE_NORMALIZER

    base = 2 * GQ + 2 * GV
    q = seg(base, base + DA) * DH_D ** -0.5
    k = seg(base + DA, base + 2 * DA)
    v = seg(base + 2 * DA, base + 3 * DA)
    feat = lax.broadcasted_iota(jnp.int32, (HD2, seq_rows), 0)
    for s in range(n_seq):
        rows = slice(s * seq_rows, (s + 1) * seq_rows)
        for hh in range(H_D):
            cols = slice(hh * HD2, (hh + 1) * HD2)
            dk_ref[s, hh] = k[rows, cols]
            dv_ref[s, hh] = v[rows, cols]
            if not prompt:
                dq_ref[s, hh] = q[rows, cols]
                continue
            qt = (q[rows, cols] * LOG2E).T
            qt_ref[s, hh, 0] = jnp.where(feat < DH_D, qt, 0.0).astype(BF16)
            qt_ref[s, hh, 1] = jnp.where(feat >= DH_D, qt, 0.0).astype(BF16)
            kb_ref[s, hh] = k[rows, cols].astype(BF16)
            for kb in range(seq_rows // ATTN_TILE):
                r0 = s * seq_rows + kb * ATTN_TILE
                vt_ref[s, hh, kb] = v[r0:r0 + ATTN_TILE, cols].T.astype(BF16)


def _proj(x2d, B, T, ln, w_main, w_gd, w_gu, b_gu, *, layer, depth, kv_slabs, emit_bf16):
    n_rows, D = x2d.shape
    if T >= ROW_TILE:
        tm, n_seq, seq_rows, tps = ROW_TILE, 1, ROW_TILE, T // ROW_TILE
    else:
        tm, n_seq, seq_rows, tps = n_rows, B, T, 1
    grid = (n_rows // tm,)
    row = lambda w: pl.BlockSpec((tm, w), lambda i: (i, 0))
    head = pl.BlockSpec((n_seq, H_D, seq_rows, HD2), lambda i: (i // tps, 0, i % tps, 0))
    slab = pl.BlockSpec((None, n_seq, H_D, seq_rows, HD2), lambda i: (layer, i // tps, 0, i % tps, 0))
    f32_rows = lambda w: jax.ShapeDtypeStruct((n_rows, w), F32)
    head_shape = lambda dt: jax.ShapeDtypeStruct((B, H_D, T, HD2), dt)
    slab_shape = jax.ShapeDtypeStruct((depth, B, H_D, T, HD2), F32)
    out_specs = [row(GQ), row(GQ), row(GV), row(GV), row(GQ), slab, slab]
    out_shape = [f32_rows(GQ), f32_rows(GQ), f32_rows(GV), f32_rows(GV), f32_rows(GQ), slab_shape, slab_shape]
    if emit_bf16:
        kt = seq_rows // ATTN_TILE
        out_specs += [pl.BlockSpec((n_seq, H_D, 2, HD2, seq_rows), lambda i: (i // tps, 0, 0, 0, i % tps)),
                      head,
                      pl.BlockSpec((n_seq, H_D, kt, HD2, ATTN_TILE), lambda i: (i // tps, 0, i % tps, 0, 0))]
        out_shape += [jax.ShapeDtypeStruct((B, H_D, 2, HD2, T), BF16), head_shape(BF16),
                      jax.ShapeDtypeStruct((B, H_D, T // ATTN_TILE, HD2, ATTN_TILE), BF16)]
    else:
        out_specs += [head]
        out_shape += [head_shape(F32)]
    args = [x2d, ln, w_main, w_gd, w_gu, b_gu]
    in_specs = [row(D), _const_spec((1, D)), _const_spec(w_main.shape), _const_spec(w_gd.shape),
                _const_spec(w_gu.shape), _const_spec((1, GQ))]
    aliases = {}
    if kv_slabs is not None:
        aliases = {len(args): 5, len(args) + 1: 6}
        args += list(kv_slabs)
        in_specs += [pl.BlockSpec(memory_space=pl.ANY)] * 2
    return pl.pallas_call(
        functools.partial(_proj_kernel, n_seq=n_seq, seq_rows=seq_rows, n_alias=len(aliases), prompt=emit_bf16),
        grid=grid,
        in_specs=in_specs,
        out_specs=out_specs,
        out_shape=out_shape,
        input_output_aliases=aliases,
        compiler_params=_params(1),
        name="proj",
    )(*args)


def _gla_kernel(q_ref, k_ref, v_ref, r_ref, la_ref, s0_ref, gn_ref, o_ref, sout_ref, s_ref, *, n_chunks, nb):
    t = pl.program_id(1)
    seqs = range(nb)

    @pl.when(t == 0)
    def _():
        for b in seqs:
            for hh in range(H_G):
                s_ref[b, hh * DK_G:(hh + 1) * DK_G, :] = s0_ref[b, hh]

    R = n_chunks * CHUNK
    ri = lax.broadcasted_iota(jnp.int32, (R, R), 0)
    ci = lax.broadcasted_iota(jnp.int32, (R, R), 1)
    causal = jnp.logical_and((ri // CHUNK) == (ci // CHUNK), ci <= ri)
    head_of_lane = lax.broadcasted_iota(jnp.int32, (R, GQ), 1) // DK_G
    tril = causal.astype(BF16)
    chunk_rows = [slice(c * CHUNK, (c + 1) * CHUNK) for c in range(n_chunks)]
    vcols = [slice(hh * DV_G, (hh + 1) * DV_G) for hh in range(H_G)]
    pad = [jnp.zeros((8 - n_chunks % 8, GQ), F32)] if n_chunks % 8 else []

    la = [la_ref[b] for b in seqs]
    a1 = [x.astype(BF16) for x in la]
    r1 = [x - y.astype(F32) for x, y in zip(la, a1)]
    a2 = [x.astype(BF16) for x in r1]
    a3 = [(x - y.astype(F32)).astype(BF16) for x, y in zip(r1, a2)]
    g = [_dot(tril, a1[b]) + _dot(tril, a2[b]) + _dot(tril, a3[b]) for b in seqs]
    last_rows = [[g[b][(c + 1) * CHUNK - 1:(c + 1) * CHUNK, :] for c in range(n_chunks)] for b in seqs]
    g_last = [jnp.concatenate([jnp.broadcast_to(x, (CHUNK, GQ)) for x in last_rows[b]], axis=0) for b in seqs]
    decay = [jnp.exp(jnp.concatenate(last_rows[b] + pad, axis=0).T) for b in seqs]
    v = [v_ref[b].astype(BF16) for b in seqs]
    q_dec = [q_ref[b] * jnp.exp(g[b]) for b in seqs]
    k_inv = [(k_ref[b] * jnp.exp(-g[b])).astype(BF16) for b in seqs]
    k_end = [k_ref[b] * jnp.exp(g_last[b] - g[b]) for b in seqs]
    qm = [[jnp.where(head_of_lane == hh, q_dec[b], 0.0).astype(BF16) for hh in range(H_G)] for b in seqs]

    o_inter = [[[] for _ in range(H_G)] for _ in seqs]
    state = [s_ref[b] for b in seqs]
    for c, rows in enumerate(chunk_rows):
        for b in seqs:
            s_b = state[b].astype(BF16)
            kt = k_end[b][rows].T.astype(BF16)
            upd = []
            for hh in range(H_G):
                o_inter[b][hh].append(_dot(qm[b][hh][rows], s_b))
                upd.append(_dot(kt[hh * DK_G:(hh + 1) * DK_G, :], v[b][rows, vcols[hh]]))
            state[b] = decay[b][:, c:c + 1] * state[b] + jnp.concatenate(upd, axis=0)
    for b in seqs:
        s_ref[b] = state[b]

    gn = gn_ref[...]
    for hh in range(H_G):
        for b in seqs:
            a = jnp.where(causal, _dot_nt(qm[b][hh], k_inv[b]), 0.0)
            o = _dot(a.astype(BF16), v[b][:, vcols[hh]]) + jnp.concatenate(o_inter[b][hh], axis=0)
            r = r_ref[b, :, vcols[hh]]
            o_ref[b, :, vcols[hh]] = _rms(o, gn) * (r * (1.0 / (1.0 + jnp.exp(-r))))

    for b in seqs:
        for hh in range(H_G):
            sout_ref[b, hh] = state[b][hh * DK_G:(hh + 1) * DK_G, :]


def _gla(gq, gk, gv, gr, la, s0, gn, B, T):
    tb = min(GLA_ROWS, T)
    nb = 4 if B % 4 == 0 else 1
    r3 = lambda a: a.reshape(B, T, a.shape[-1])
    blk = lambda w: pl.BlockSpec((nb, tb, w), lambda b, t: (b, t, 0))
    st = pl.BlockSpec((nb, H_G, DK_G, DV_G), lambda b, t: (b, 0, 0, 0))
    return pl.pallas_call(
        functools.partial(_gla_kernel, n_chunks=tb // CHUNK, nb=nb),
        grid=(B // nb, T // tb),
        in_specs=[blk(GQ), blk(GQ), blk(GV), blk(GV), blk(GQ), st, _const_spec((1, DV_G))],
        out_specs=[blk(GV), st],
        out_shape=[jax.ShapeDtypeStruct((B, T, GV), F32),
                   jax.ShapeDtypeStruct((B, H_G, DK_G, DV_G), F32)],
        scratch_shapes=[pltpu.VMEM((nb, GQ, DV_G), F32)],
        compiler_params=_params(2),
        name="gla",
    )(r3(gq), r3(gk), r3(gv), r3(gr), r3(la), s0, gn)


def _lam(lp_ref, lam0):
    lp = lp_ref[...]
    t1 = jnp.sum(lp[0:1] * lp[1:2], axis=-1, keepdims=True)
    t2 = jnp.sum(lp[2:3] * lp[3:4], axis=-1, keepdims=True)
    return jnp.exp(t1) - jnp.exp(t2) + lam0


def _split_maps(q):
    lane = lax.broadcasted_iota(jnp.int32, q.shape, 1)
    return (jnp.where(lane < DH_D, q, 0.0).astype(BF16),
            jnp.where(lane >= DH_D, q, 0.0).astype(BF16))


def _attn_finish(acc0, l0, acc1, l1, lam, dn, lam0):
    o = acc0 * (1.0 / l0) - lam * (acc1 * (1.0 / l1))
    return _rms(o, dn) * (1.0 - lam0)


def _attn_prompt_kernel(lp_ref, q_ref, k_ref, vt_ref, bias_ref, dn_ref, o_ref,
                        s_ref, p_ref, alpha_ref, m_ref, l_ref, acc_ref, *, tile, lam0, nh):
    i = pl.program_id(2)
    m_ref[...] = jnp.full(m_ref.shape, NEG_INIT, F32)
    l_ref[...] = jnp.zeros(l_ref.shape, F32)
    acc_ref[...] = jnp.zeros(acc_ref.shape, F32)
    both = (0, 1)
    heads = range(nh)

    def scores(j, slot, maps=both):
        ks = pl.multiple_of(j * tile, tile)
        for hd in heads:
            kt = k_ref[0, hd, pl.ds(ks, tile), :]
            for m in maps:
                s_ref[hd, slot, m] = _dot(kt, q_ref[0, hd, m])

    def weighted_values(j, slot, maps=both):
        for hd in heads:
            vt = vt_ref[0, hd, j]
            for m in maps:
                acc_ref[hd, m] = alpha_ref[hd, slot, m] * acc_ref[hd, m] + _dot(vt, p_ref[hd, slot, m])

    def softmax_map(slot, near, m):
        for hd in heads:
            s = s_ref[hd, slot, m]
            if near is not None:
                s = s + bias_ref[hd, near]
            m_old = m_ref[hd, m]
            m_new = jnp.maximum(m_old, jnp.max(s, axis=0, keepdims=True))
            p = jnp.exp2(s - m_new)
            p_ref[hd, slot, m] = p.astype(BF16)
            alpha = jnp.exp2(m_old - m_new)
            alpha_ref[hd, slot, m] = alpha
            l_ref[hd, m] = alpha * l_ref[hd, m] + jnp.sum(p, axis=0, keepdims=True)
            m_ref[hd, m] = m_new

    def step(r, slot, *, near=None, first=False, last=False):
        nxt = jnp.maximum(i - r - 1, 0)
        if first:
            scores(i, slot, maps=(1,))
        if not last:
            scores(nxt, 1 - slot, maps=(0,))
        if not first:
            weighted_values(i - r + 1, 1 - slot, maps=(0,))
        softmax_map(slot, near, 0)
        if not last:
            scores(nxt, 1 - slot, maps=(1,))
        if not first:
            weighted_values(i - r + 1, 1 - slot, maps=(1,))
        if last:
            weighted_values(i - r, slot, maps=(0,))
        softmax_map(slot, near, 1)
        if last:
            weighted_values(i - r, slot, maps=(1,))

    scores(i, 0, maps=(0,))
    for last in (False, True):
        pl.when((i == 0) == last)(functools.partial(step, 0, 0, near=0, first=True, last=last))
    for last in (False, True):
        pl.when(jnp.logical_and(i >= 1, (i == 1) == last))(functools.partial(step, 1, 1, near=1, last=last))

    def far_pair(c, carry):
        r = 2 * c + 2
        step(r, 0)
        step(r + 1, 1)
        return carry

    n_mid = jnp.maximum(i - 2, 0)
    lax.fori_loop(0, n_mid // 2, far_pair, 0)
    pl.when(jnp.logical_and(i >= 3, i % 2 == 1))(functools.partial(step, i - 1, 0))
    pl.when(jnp.logical_and(i >= 2, i % 2 == 0))(functools.partial(step, i, 0, last=True))
    pl.when(jnp.logical_and(i >= 3, i % 2 == 1))(functools.partial(step, i, 1, last=True))

    lam = _lam(lp_ref, lam0)
    for hd in heads:
        ot = acc_ref[hd, 0] * (1.0 / l_ref[hd, 0]) - lam * (acc_ref[hd, 1] * (1.0 / l_ref[hd, 1]))
        scale = lax.rsqrt(jnp.mean(ot * ot, axis=0, keepdims=True) + EPS) * (1.0 - lam0)
        o_ref[0, hd] = ot * scale * dn_ref[...]


def _attn_prompt(lp, qt, kb, vtb, bias, dn_col, lam0):
    B, H, _, _, T = qt.shape
    tile = ATTN_TILE
    nh = 2 if H % 2 == 0 else 1
    return pl.pallas_call(
        functools.partial(_attn_prompt_kernel, tile=tile, lam0=lam0, nh=nh),
        grid=(B, H // nh, T // tile),
        in_specs=[_const_spec(lp.shape),
                  pl.BlockSpec((1, nh, 2, HD2, tile), lambda b, h, i: (b, h, 0, 0, i)),
                  pl.BlockSpec((1, nh, T, HD2), lambda b, h, i: (b, h, 0, 0)),
                  pl.BlockSpec((1, nh, T // tile, HD2, tile), lambda b, h, i: (b, h, 0, 0, 0)),
                  pl.BlockSpec((nh, 2, tile, tile), lambda b, h, i: (h, 0, 0, 0)),
                  _const_spec((HD2, 1))],
        out_specs=pl.BlockSpec((1, nh, HD2, tile), lambda b, h, i: (b, h, 0, i)),
        out_shape=jax.ShapeDtypeStruct((B, H, HD2, T), F32),
        scratch_shapes=[pltpu.VMEM((nh, 2, 2, tile, tile), F32),
                        pltpu.VMEM((nh, 2, 2, tile, tile), BF16),
                        pltpu.VMEM((nh, 2, 2, 1, tile), F32),
                        pltpu.VMEM((nh, 2, 1, tile), F32), pltpu.VMEM((nh, 2, 1, tile), F32),
                        pltpu.VMEM((nh, 2, HD2, tile), F32)],
        compiler_params=_params(3),
        name="attn_prompt",
    )(lp, qt, kb, vtb, bias, dn_col)


def _attn_step_kernel(lp_ref, q_ref, kp_ref, vp_ref, kn_ref, vn_ref, bias_ref, dn_ref, o_ref, *, past, lam0):
    lam = _lam(lp_ref, lam0)
    dn = dn_ref[...]
    for hh in range(H_D):
        qm = _split_maps(q_ref[0, hh])
        kp = kp_ref[0, hh].astype(BF16)
        vp = vp_ref[0, hh].astype(BF16)
        kn = kn_ref[0, hh].astype(BF16)
        vn = vn_ref[0, hh].astype(BF16)
        bias_p = bias_ref[hh, 0, :, :past]
        bias_n = bias_ref[hh, 0, :, past:]
        res = []
        for m in range(2):
            s_p = _dot_nt(qm[m], kp) + bias_p
            s_n = _dot_nt(qm[m], kn) + bias_n
            mx = jnp.maximum(jnp.max(s_p, axis=-1, keepdims=True), jnp.max(s_n, axis=-1, keepdims=True))
            p_p = jnp.exp(s_p - mx)
            p_n = jnp.exp(s_n - mx)
            l = jnp.sum(p_p, axis=-1, keepdims=True) + jnp.sum(p_n, axis=-1, keepdims=True)
            acc = _dot(p_p.astype(BF16), vp) + _dot(p_n.astype(BF16), vn)
            res += [acc, l]
        o_ref[0, :, hh * HD2:(hh + 1) * HD2] = _attn_finish(res[0], res[1], res[2], res[3], lam, dn, lam0)


def _attn_step(lp, dq, k_past, v_past, dk, dv, bias, dn, lam0, layer):
    B, H, T, _ = dq.shape
    past = k_past.shape[3]
    qblk = pl.BlockSpec((1, H, T, HD2), lambda b: (b, 0, 0, 0))
    new = pl.BlockSpec((None, 1, H, T, HD2), lambda b: (layer, b, 0, 0, 0))
    old = pl.BlockSpec((None, 1, H, past, HD2), lambda b: (layer, b, 0, 0, 0))
    return pl.pallas_call(
        functools.partial(_attn_step_kernel, past=past, lam0=lam0),
        grid=(B,),
        in_specs=[_const_spec(lp.shape), qblk, old, old, new, new, _const_spec(bias.shape), _const_spec((1, HD2))],
        out_specs=pl.BlockSpec((1, T, H * HD2), lambda b: (b, 0, 0)),
        out_shape=jax.ShapeDtypeStruct((B, T, H * HD2), F32),
        compiler_params=_params(1),
        name="attn_step",
    )(lp, dq, k_past, v_past, dk, dv, bias, dn)


SUBLANES = 8
LANES = 128
HEADER = 2 * SUBLANES


def _interleave_pitch(nv):
    p = nv + SUBLANES
    return p if (p // SUBLANES) % 2 else p + SUBLANES


def _ffn_kernel(x_ref, og_ref, od_ref, wo_ref, lpost_ref, lpre_ref, lout_ref, wup_ref, cw_ref, cb_ref,
                cs_ref, wdn_ref, y_ref, cnew_ref, prev_ref, ext_ref, act_ref, tbuf_ref, pbuf_ref,
                *, n_seq, seq_rows, tiles_per_seq, d_ff, fc, od_transposed):
    ti = pl.program_id(0) % tiles_per_seq
    tm, D = x_ref.shape
    nv = seq_rows // SUBLANES
    pitch = _interleave_pitch(nv)
    n_slabs = D // LANES

    @pl.when(ti == 0)
    def _():
        prev_ref[...] = cs_ref[...]

    if od_transposed:
        od = jnp.concatenate([od_ref[0, hh].T for hh in range(H_D)], axis=1)
    else:
        od = od_ref[...]
    mix = _dot(og_ref[...].astype(BF16), wo_ref[:GV, :]) + _dot(od.astype(BF16), wo_ref[GV:, :])
    x1 = x_ref[...] + _rms(mix, lpost_ref[...])
    h = _rms(x1, lpre_ref[...])

    for l in range(n_slabs):
        for q in range(n_seq):
            for s in range(SUBLANES):
                t0 = q * seq_rows + nv * s
                tbuf_ref[l, pl.ds((q * SUBLANES + s) * pitch, nv), :] = h[t0:t0 + nv, l * LANES:(l + 1) * LANES]
    for l in range(n_slabs):
        for q in range(n_seq):
            for v in range(nv):
                r0 = q * seq_rows + SUBLANES * v
                pbuf_ref[r0:r0 + SUBLANES, l * LANES:(l + 1) * LANES] = \
                    tbuf_ref[l, pl.ds(q * SUBLANES * pitch + v, SUBLANES, stride=pitch), :]
    hp = pbuf_ref[...].astype(BF16)

    n_chunks = d_ff // fc
    split = (n_chunks + 1) // 2 + 1
    k_gelu = math.sqrt(2.0 / math.pi)
    first_row = lax.broadcasted_iota(jnp.int32, (SUBLANES, fc), 0) == 0

    def up_pair(c):
        return [_dot(hp, wup_ref[:, off:off + fc]) for off in (c * fc, d_ff + c * fc)]

    def header(block, carried):
        return jnp.where(first_row, carried, pltpu.roll(block, 1, 0))

    nxt = up_pair(0)
    for c in range(n_chunks):
        cur = nxt
        if c + 1 < n_chunks:
            nxt = up_pair(c + 1)
        halves = []
        for hf, off in enumerate((c * fc, d_ff + c * fc)):
            cols = slice(off, off + fc)
            buf = ext_ref.at[c % 2, hf]
            cw = cw_ref[:, cols]
            cb = cb_ref[:, cols]
            parts = []
            for q in range(n_seq):
                up = cur[hf][q * seq_rows:(q + 1) * seq_rows, :]
                base = q * (seq_rows + HEADER)
                buf[base:base + SUBLANES, :] = header(up[seq_rows - 2 * SUBLANES:seq_rows - SUBLANES, :],
                                                      prev_ref[q, 0:1, cols])
                buf[base + SUBLANES:base + HEADER, :] = header(up[seq_rows - SUBLANES:, :], prev_ref[q, 1:2, cols])
                buf[base + HEADER:base + HEADER + seq_rows, :] = up
                d2 = buf[base:base + seq_rows, :]
                d1 = buf[base + SUBLANES:base + SUBLANES + seq_rows, :]
                parts.append(cb + cw[0:1] * d2 + cw[1:2] * d1 + cw[2:3] * up)
                prev_ref[q, 0:1, cols] = up[seq_rows - SUBLANES - 1:seq_rows - SUBLANES, :]
                prev_ref[q, 1:2, cols] = up[seq_rows - 1:seq_rows, :]
            halves.append(parts[0] if n_seq == 1 else jnp.concatenate(parts, axis=0))
        g, u = halves
        gelu = g * (0.5 + 0.5 * jnp.tanh(g * (k_gelu + (k_gelu * 0.044715) * (g * g))))
        act_ref[:, c * fc:(c + 1) * fc] = (gelu * u).astype(BF16)
        if c + 1 == split:
            down = _dot(act_ref[:, :split * fc], wdn_ref[:split * fc, :])
    down = down + _dot(act_ref[:, split * fc:], wdn_ref[split * fc:, :])

    for l in range(n_slabs):
        tbuf_ref[l, pl.ds(0, tm), :] = down[:, l * LANES:(l + 1) * LANES]
    for l in range(n_slabs):
        for q in range(n_seq):
            for s in range(SUBLANES):
                for j in range(nv // SUBLANES):
                    t0 = q * seq_rows + nv * s + SUBLANES * j
                    pbuf_ref[t0:t0 + SUBLANES, l * LANES:(l + 1) * LANES] = \
                        tbuf_ref[l, pl.ds(q * seq_rows + SUBLANES * SUBLANES * j + s, SUBLANES, stride=SUBLANES), :]
    y_ref[...] = x1 + _rms(pbuf_ref[...], lout_ref[...])
    cnew_ref[...] = prev_ref[...]


def _ffn(x2d, og2d, od, B, T, w_out, lpost, lpre, lout, w_up, conv_w, conv_b, conv_state, w_down):
    n_rows, D = x2d.shape
    f2 = w_up.shape[1]
    d_ff = f2 // 2
    if T >= ROW_TILE:
        tm, n_seq, seq_rows, tps = ROW_TILE, 1, ROW_TILE, T // ROW_TILE
    else:
        tm, n_seq, seq_rows, tps = n_rows, B, T, 1
    row = lambda w: pl.BlockSpec((tm, w), lambda i: (i, 0))
    cs = pl.BlockSpec((n_seq, CONV_W - 1, f2), lambda i: (i // tps, 0, 0))
    od_transposed = od.ndim == 4
    if od_transposed:
        assert n_seq == 1
        od_spec = pl.BlockSpec((1, H_D, HD2, tm), lambda i: (i // tps, 0, 0, i % tps))
    else:
        od_spec = row(DA)
    fc = 256
    return pl.pallas_call(
        functools.partial(_ffn_kernel, n_seq=n_seq, seq_rows=seq_rows, tiles_per_seq=tps, d_ff=d_ff, fc=fc,
                          od_transposed=od_transposed),
        grid=(n_rows // tm,),
        in_specs=[row(D), row(GV), od_spec, _const_spec(w_out.shape), _const_spec((1, D)), _const_spec((1, D)),
                  _const_spec((1, D)), _const_spec(w_up.shape), _const_spec(conv_w.shape),
                  _const_spec((1, f2)), cs, _const_spec(w_down.shape)],
        out_specs=[row(D), cs],
        out_shape=[jax.ShapeDtypeStruct((n_rows, D), F32),
                   jax.ShapeDtypeStruct((B, CONV_W - 1, f2), F32)],
        scratch_shapes=[pltpu.VMEM((n_seq, CONV_W - 1, f2), F32),
                        pltpu.VMEM((2, 2, n_seq * (seq_rows + HEADER), fc), F32),
                        pltpu.VMEM((tm, d_ff), BF16),
                        pltpu.VMEM((D // LANES, max(tm, n_seq * SUBLANES * _interleave_pitch(seq_rows // SUBLANES)),
                                    LANES), F32),
                        pltpu.VMEM((tm, D), F32)],
        compiler_params=_params(1),
        name="ffn",
    )(x2d, og2d, od, w_out, lpost, lpre, lout, w_up, conv_w, conv_b, conv_state, w_down)


def _lambda_init(layer):
    return 0.8 - 0.6 * math.exp(-0.3 * layer)


def _layer(x2d, B, T, layer, depth, kv_slabs, past_k, past_v, gla_state, conv_state, bias, wts):
    (w_main, w_gd, w_gu, b_gu, gn, lp, dn, w_out, ln_pre, ln_post, lf_pre, lf_post,
     w_up, conv_w, conv_b, w_down) = wts
    prompt = past_k is None
    outs = _proj(x2d, B, T, ln_pre, w_main, w_gd, w_gu, b_gu, layer=layer, depth=depth, kv_slabs=kv_slabs,
                 emit_bf16=prompt)
    gq, gk, gv, gr, la, dk, dv = outs[:7]
    og, s_new = _gla(gq, gk, gv, gr, la, gla_state, gn, B, T)
    lam0 = _lambda_init(layer)
    if prompt:
        qt, kb, vtb = outs[7:]
        od = _attn_prompt(lp, qt, kb, vtb, bias, dn.reshape(-1, 1), lam0)
    else:
        od = _attn_step(lp, outs[7], past_k, past_v, dk, dv, bias, dn, lam0, layer).reshape(B * T, DA)
    y, c_new = _ffn(x2d, og.reshape(B * T, GV), od, B, T, w_out, ln_post, lf_pre, lf_post,
                    w_up, conv_w, conv_b, conv_state, w_down)
    return y, (dk, dv), s_new, c_new


def kernel(x_prompt, x_sample, cache_k, cache_v, state_gla, state_conv, t5_table, w_in, w_gate_up, b_gate_up,
           gla_norm, lam_params, diff_norm, w_out, ln_mix_pre, ln_mix_post, ln_ffn_pre, ln_ffn_post,
           w_ffn_up, conv_w, conv_b, w_ffn_down):
    depth = w_in.shape[0]
    B, T, D = x_prompt.shape
    Bs, Ts, _ = x_sample.shape
    past = cache_k.shape[3]
    f2 = w_ffn_up.shape[2]
    gd0 = 2 * GQ + 2 * GV

    def layer_weights(l):
        row = lambda a: a[l].reshape(1, -1)
        w_main = jnp.concatenate([w_in[l, :, :gd0], w_in[l, :, gd0 + GATE_RANK:]], axis=1).astype(BF16)
        return (w_main, w_in[l, :, gd0:gd0 + GATE_RANK].astype(BF16), w_gate_up[l].astype(BF16),
                row(b_gate_up), row(gla_norm), lam_params[l], row(diff_norm), w_out[l].astype(BF16),
                row(ln_mix_pre), row(ln_mix_post), row(ln_ffn_pre), row(ln_ffn_post),
                w_ffn_up[l].astype(BF16), conv_w[l], row(conv_b), w_ffn_down[l].astype(BF16))

    wts = [layer_weights(l) for l in range(depth)]
    bias_prompt = _bias_tiles(t5_table, [0, ATTN_TILE], ATTN_TILE, ATTN_TILE, keys_on_rows=True, exp2_shifted=True)
    bias_step = _bias_tiles(t5_table, [past], Ts, past + Ts)

    y = x_prompt.reshape(B * T, D)
    kv_p, gp, cp = None, [], []
    zero_s = jnp.zeros((B, H_G, DK_G, DV_G), F32)
    zero_c = jnp.zeros((B, CONV_W - 1, f2), F32)
    for l in range(depth):
        y, kv_p, s_new, c_new = _layer(y, B, T, l, depth, kv_p, None, None, zero_s, zero_c, bias_prompt, wts[l])
        gp.append(s_new); cp.append(c_new)
    y_prompt = y.reshape(B, T, D)

    y = x_sample.reshape(Bs * Ts, D)
    kv_s, gs, cs = None, [], []
    for l in range(depth):
        y, kv_s, s_new, c_new = _layer(y, Bs, Ts, l, depth, kv_s, cache_k, cache_v, state_gla[l],
                                       state_conv[l], bias_step, wts[l])
        gs.append(s_new); cs.append(c_new)
    y_sample = y.reshape(Bs, Ts, D)

    return (y_prompt, y_sample, kv_p[0], kv_p[1], jnp.stack(gp), jnp.stack(cp),
            kv_s[0], kv_s[1], jnp.stack(gs), jnp.stack(cs))
```

```python
import functools
import math

import jax
import jax.numpy as jnp
from jax import lax
from jax.experimental import pallas as pl
from jax.experimental.pallas import tpu as pltpu

F32 = jnp.float32
BF16 = jnp.bfloat16

CHUNK = 64
H_G = 4
DK_G = 64
DV_G = 128
GATE_RANK = 16
GATE_NORMALIZER = 16.0
H_D = 4
DH_D = 64
NUM_BUCKETS = 32
MAX_DISTANCE = 128
CONV_W = 3
EPS = 1e-6

GQ = H_G * DK_G
GV = H_G * DV_G
DA = H_D * 2 * DH_D
HD2 = 2 * DH_D

VMEM_LIMIT_BYTES = 56 * 1024 * 1024
ROW_TILE = 512
GLA_ROWS = 256
ATTN_TILE = 512
ONES_ROWS = 16
NEG_INIT = -1e30
LOG2E = math.log2(math.e)


def _rms(x, w):
    return x * lax.rsqrt(jnp.mean(x * x, axis=-1, keepdims=True) + EPS) * w


def _dot(a, b):
    return jnp.dot(a, b, preferred_element_type=F32)


def _dot_nt(a, b):
    return lax.dot_general(a, b, (((1,), (1,)), ((), ())), preferred_element_type=F32)


def _params(n_grid):
    return pltpu.CompilerParams(dimension_semantics=("arbitrary",) * n_grid,
                                vmem_limit_bytes=VMEM_LIMIT_BYTES)


def _const_spec(shape):
    nd = len(shape)
    return pl.BlockSpec(shape, lambda *_: (0,) * nd, pipeline_mode=pl.Buffered(1))


def _layer_weight_spec(w, layer):
    return pl.BlockSpec((None,) + w.shape[1:], lambda *_: (layer, 0, 0), pipeline_mode=pl.Buffered(1))


def _bias_kernel(tab_ref, o_ref, *, qstarts, nq, nk, keys_on_rows, exp2_shifted):
    h = pl.program_id(0)
    nb = NUM_BUCKETS // 2
    max_exact = nb // 2
    shape = (nk, nq) if keys_on_rows else (nq, nk)
    q_axis, k_axis = (1, 0) if keys_on_rows else (0, 1)
    for d, q0 in enumerate(qstarts):
        qpos = q0 + lax.broadcasted_iota(jnp.int32, shape, q_axis)
        kpos = lax.broadcasted_iota(jnp.int32, shape, k_axis)
        rel = kpos - qpos
        ret = jnp.where(rel > 0, nb, 0)
        n = jnp.abs(rel)
        assert (nb, max_exact, MAX_DISTANCE) == (16, 8, 128)
        nn = jnp.maximum(n * n, 1)
        large = jnp.minimum(33 - lax.clz(nn), nb - 1)
        bucket = ret + jnp.where(n < max_exact, n, large)
        bias = jnp.zeros(shape, F32)
        for b in range(NUM_BUCKETS):
            bias = jnp.where(bucket == b, tab_ref[b, h], bias)
        if exp2_shifted:
            bias = (bias - tab_ref[nb - 1, h]) * LOG2E
        visible = (kpos // CHUNK) <= (qpos // CHUNK)
        o_ref[0, d] = jnp.where(visible, bias, -jnp.inf)


def _bias_tiles(t5_table, qstarts, nq, nk, *, keys_on_rows=False, exp2_shifted=False):
    nd = len(qstarts)
    shape = (nk, nq) if keys_on_rows else (nq, nk)
    return pl.pallas_call(
        functools.partial(_bias_kernel, qstarts=tuple(qstarts), nq=nq, nk=nk, keys_on_rows=keys_on_rows,
                          exp2_shifted=exp2_shifted),
        grid=(H_D,),
        in_specs=[pl.BlockSpec(memory_space=pltpu.SMEM)],
        out_specs=pl.BlockSpec((1, nd) + shape, lambda h: (h, 0, 0, 0)),
        out_shape=jax.ShapeDtypeStruct((H_D, nd) + shape, F32),
        compiler_params=_params(1),
        name="t5_bias_tiles",
    )(t5_table)


def _proj_kernel(*refs, n_seq, seq_rows, n_alias, prompt, layer, depth):
    x_ref, ln_ref, w_ref, wgd_ref, wgu_ref, bgu_ref = refs[:6]
    gq_ref, gk_ref, gv_ref, gr_ref, la_ref, dk_ref, dv_ref = refs[6 + n_alias:13 + n_alias]
    if n_alias == 0:
        for other in range(depth):
            if other != layer:
                dk_ref[other] = jnp.zeros(dk_ref.shape[1:], F32)
                dv_ref[other] = jnp.zeros(dv_ref.shape[1:], F32)
        dk_ref, dv_ref = dk_ref.at[layer], dv_ref.at[layer]
    if prompt:
        qt_ref, kb_ref, vt_ref = refs[13 + n_alias:]
    else:
        dq_ref, = refs[13 + n_alias:]
    h = _rms(x_ref[...], ln_ref[...]).astype(BF16)

    def seg(a, b):
        return _dot(h, w_ref[:, a:b])

    base = 2 * GQ + 2 * GV
    q = seg(base, base + DA) * DH_D ** -0.5
    k = seg(base + DA, base + 2 * DA)
    v = seg(base + 2 * DA, base + 3 * DA)
    gd = _dot(h, wgd_ref[...]).astype(BF16)
    z = _dot(gd, wgu_ref[...]) + bgu_ref[...]
    gq_ref[...] = seg(0, GQ) * DK_G ** -0.5
    gk_ref[...] = seg(GQ, 2 * GQ)
    gv_ref[...] = seg(2 * GQ, 2 * GQ + GV)
    gr_ref[...] = seg(2 * GQ + GV, 2 * GQ + 2 * GV)
    log_sig = jnp.minimum(z, 0.0) - jnp.log1p(jnp.exp(-jnp.abs(z)))
    la_ref[...] = log_sig / GATE_NORMALIZER
    feat = lax.broadcasted_iota(jnp.int32, (HD2, seq_rows), 0)
    for s in range(n_seq):
        rows = slice(s * seq_rows, (s + 1) * seq_rows)
        for hh in range(H_D):
            cols = slice(hh * HD2, (hh + 1) * HD2)
            dk_ref[s, hh] = k[rows, cols]
            dv_ref[s, hh] = v[rows, cols]
            if not prompt:
                dq_ref[s, hh] = q[rows, cols]
                continue
            qt = (q[rows, cols] * LOG2E).T
            qt_ref[s, hh, 0] = jnp.where(feat < DH_D, qt, 0.0).astype(BF16)
            qt_ref[s, hh, 1] = jnp.where(feat >= DH_D, qt, 0.0).astype(BF16)
            kb_ref[s, hh] = k[rows, cols].astype(BF16)
            for kb in range(seq_rows // ATTN_TILE):
                r0 = s * seq_rows + kb * ATTN_TILE
                vt_ref[s, hh, kb, :HD2] = v[r0:r0 + ATTN_TILE, cols].T.astype(BF16)
                vt_ref[s, hh, kb, HD2:] = jnp.ones((ONES_ROWS, ATTN_TILE), BF16)


def _proj(x2d, B, T, ln, w_main, w_gd, w_gu, b_gu, *, layer, depth, kv_slabs, emit_bf16):
    n_rows, D = x2d.shape
    if T >= ROW_TILE:
        tm, n_seq, seq_rows, tps = ROW_TILE, 1, ROW_TILE, T // ROW_TILE
    else:
        tm, n_seq, seq_rows, tps = n_rows, B, T, 1
    grid = (n_rows // tm,)
    row = lambda w: pl.BlockSpec((tm, w), lambda i: (i, 0))
    head = pl.BlockSpec((n_seq, H_D, seq_rows, HD2), lambda i: (i // tps, 0, i % tps, 0))
    if kv_slabs is None:
        slab = pl.BlockSpec((depth, n_seq, H_D, seq_rows, HD2), lambda i: (0, i // tps, 0, i % tps, 0))
    else:
        slab = pl.BlockSpec((None, n_seq, H_D, seq_rows, HD2), lambda i: (layer, i // tps, 0, i % tps, 0))
    f32_rows = lambda w: jax.ShapeDtypeStruct((n_rows, w), F32)
    head_shape = lambda dt: jax.ShapeDtypeStruct((B, H_D, T, HD2), dt)
    slab_shape = jax.ShapeDtypeStruct((depth, B, H_D, T, HD2), F32)
    out_specs = [row(GQ), row(GQ), row(GV), row(GV), row(GQ), slab, slab]
    out_shape = [f32_rows(GQ), f32_rows(GQ), f32_rows(GV), f32_rows(GV), f32_rows(GQ), slab_shape, slab_shape]
    if emit_bf16:
        kt = seq_rows // ATTN_TILE
        out_specs += [pl.BlockSpec((n_seq, H_D, 2, HD2, seq_rows), lambda i: (i // tps, 0, 0, 0, i % tps)),
                      head,
                      pl.BlockSpec((n_seq, H_D, kt, HD2 + ONES_ROWS, ATTN_TILE),
                                   lambda i: (i // tps, 0, i % tps, 0, 0))]
        out_shape += [jax.ShapeDtypeStruct((B, H_D, 2, HD2, T), BF16), head_shape(BF16),
                      jax.ShapeDtypeStruct((B, H_D, T // ATTN_TILE, HD2 + ONES_ROWS, ATTN_TILE), BF16)]
    else:
        out_specs += [head]
        out_shape += [head_shape(F32)]
    args = [x2d, ln, w_main, w_gd, w_gu, b_gu]
    in_specs = [row(D), _const_spec((1, D)), _layer_weight_spec(w_main, layer), _layer_weight_spec(w_gd, layer),
                _layer_weight_spec(w_gu, layer), _const_spec((1, GQ))]
    aliases = {}
    if kv_slabs is not None:
        aliases = {len(args): 5, len(args) + 1: 6}
        args += list(kv_slabs)
        in_specs += [pl.BlockSpec(memory_space=pl.ANY)] * 2
    return pl.pallas_call(
        functools.partial(_proj_kernel, n_seq=n_seq, seq_rows=seq_rows, n_alias=len(aliases), prompt=emit_bf16,
                          layer=layer, depth=depth),
        grid=grid,
        in_specs=in_specs,
        out_specs=out_specs,
        out_shape=out_shape,
        input_output_aliases=aliases,
        compiler_params=_params(1),
        name="proj",
    )(*args)


def _gla_kernel(q_ref, k_ref, v_ref, r_ref, la_ref, s0_ref, gn_ref, o_ref, sout_ref, s_ref, *, n_chunks, nb):
    t = pl.program_id(1)
    seqs = range(nb)

    @pl.when(t == 0)
    def _():
        for b in seqs:
            for hh in range(H_G):
                s_ref[b, hh * DK_G:(hh + 1) * DK_G, :] = s0_ref[b, hh]

    R = n_chunks * CHUNK
    ri = lax.broadcasted_iota(jnp.int32, (R, R), 0)
    ci = lax.broadcasted_iota(jnp.int32, (R, R), 1)
    causal = jnp.logical_and((ri // CHUNK) == (ci // CHUNK), ci <= ri)
    head_of_lane = lax.broadcasted_iota(jnp.int32, (R, GQ), 1) // DK_G
    tril = causal.astype(BF16)
    chunk_rows = [slice(c * CHUNK, (c + 1) * CHUNK) for c in range(n_chunks)]
    vcols = [slice(hh * DV_G, (hh + 1) * DV_G) for hh in range(H_G)]
    pad = [jnp.zeros((8 - n_chunks % 8, GQ), F32)] if n_chunks % 8 else []

    la = [la_ref[b] for b in seqs]
    a1 = [x.astype(BF16) for x in la]
    r1 = [x - y.astype(F32) for x, y in zip(la, a1)]
    a2 = [x.astype(BF16) for x in r1]
    a3 = [(x - y.astype(F32)).astype(BF16) for x, y in zip(r1, a2)]
    g = [_dot(tril, a1[b]) + _dot(tril, a2[b]) + _dot(tril, a3[b]) for b in seqs]
    last_rows = [[g[b][(c + 1) * CHUNK - 1:(c + 1) * CHUNK, :] for c in range(n_chunks)] for b in seqs]
    g_last = [jnp.concatenate([jnp.broadcast_to(x, (CHUNK, GQ)) for x in last_rows[b]], axis=0) for b in seqs]
    decay = [jnp.exp(jnp.concatenate(last_rows[b] + pad, axis=0).T) for b in seqs]
    v = [v_ref[b].astype(BF16) for b in seqs]
    q_dec = [q_ref[b] * jnp.exp(g[b]) for b in seqs]
    k_inv = [(k_ref[b] * jnp.exp(-g[b])).astype(BF16) for b in seqs]
    k_end = [k_ref[b] * jnp.exp(g_last[b] - g[b]) for b in seqs]
    qm = [[jnp.where(head_of_lane == hh, q_dec[b], 0.0).astype(BF16) for hh in range(H_G)] for b in seqs]

    o_inter = [[[] for _ in range(H_G)] for _ in seqs]
    state = [s_ref[b] for b in seqs]
    for c, rows in enumerate(chunk_rows):
        for b in seqs:
            s_b = state[b].astype(BF16)
            kt = k_end[b][rows].T.astype(BF16)
            upd = []
            for hh in range(H_G):
                o_inter[b][hh].append(_dot(qm[b][hh][rows], s_b))
                upd.append(_dot(kt[hh * DK_G:(hh + 1) * DK_G, :], v[b][rows, vcols[hh]]))
            state[b] = decay[b][:, c:c + 1] * state[b] + jnp.concatenate(upd, axis=0)
    for b in seqs:
        s_ref[b] = state[b]

    gn = gn_ref[...]
    for hh in range(H_G):
        for b in seqs:
            a = jnp.where(causal, _dot_nt(qm[b][hh], k_inv[b]), 0.0)
            o = _dot(a.astype(BF16), v[b][:, vcols[hh]]) + jnp.concatenate(o_inter[b][hh], axis=0)
            r = r_ref[b, :, vcols[hh]]
            o_ref[b, :, vcols[hh]] = _rms(o, gn) * (r * (1.0 / (1.0 + jnp.exp(-r))))

    for b in seqs:
        for hh in range(H_G):
            sout_ref[b, hh] = state[b][hh * DK_G:(hh + 1) * DK_G, :]


def _gla(gq, gk, gv, gr, la, s0, gn, B, T):
    tb = min(GLA_ROWS, T)
    nb = 4 if B % 4 == 0 else 1
    r3 = lambda a: a.reshape(B, T, a.shape[-1])
    blk = lambda w: pl.BlockSpec((nb, tb, w), lambda b, t: (b, t, 0))
    st = pl.BlockSpec((nb, H_G, DK_G, DV_G), lambda b, t: (b, 0, 0, 0))
    return pl.pallas_call(
        functools.partial(_gla_kernel, n_chunks=tb // CHUNK, nb=nb),
        grid=(B // nb, T // tb),
        in_specs=[blk(GQ), blk(GQ), blk(GV), blk(GV), blk(GQ), st, _const_spec((1, DV_G))],
        out_specs=[blk(GV), st],
        out_shape=[jax.ShapeDtypeStruct((B, T, GV), F32),
                   jax.ShapeDtypeStruct((B, H_G, DK_G, DV_G), F32)],
        scratch_shapes=[pltpu.VMEM((nb, GQ, DV_G), F32)],
        compiler_params=_params(2),
        name="gla",
    )(r3(gq), r3(gk), r3(gv), r3(gr), r3(la), s0, gn)


def _lam(lp_ref, lam0):
    lp = lp_ref[...]
    t1 = jnp.sum(lp[0:1] * lp[1:2], axis=-1, keepdims=True)
    t2 = jnp.sum(lp[2:3] * lp[3:4], axis=-1, keepdims=True)
    return jnp.exp(t1) - jnp.exp(t2) + lam0


def _split_maps(q):
    lane = lax.broadcasted_iota(jnp.int32, q.shape, 1)
    return (jnp.where(lane < DH_D, q, 0.0).astype(BF16),
            jnp.where(lane >= DH_D, q, 0.0).astype(BF16))


def _attn_finish(acc0, l0, acc1, l1, lam, dn, lam0):
    o = acc0 * (1.0 / l0) - lam * (acc1 * (1.0 / l1))
    return _rms(o, dn) * (1.0 - lam0)


def _attn_prompt_kernel(lp_ref, q_ref, k_ref, vt_ref, bias_ref, dn_ref, o_ref,
                        s_ref, p_ref, alpha_ref, m_ref, acc_ref, *, tile, lam0, nh):
    i = pl.program_id(2)
    m_ref[...] = jnp.full(m_ref.shape, NEG_INIT, F32)
    acc_ref[...] = jnp.zeros(acc_ref.shape, F32)
    both = (0, 1)
    heads = range(nh)

    def scores(j, slot, maps=both):
        ks = pl.multiple_of(j * tile, tile)
        for hd in heads:
            kt = k_ref[0, hd, pl.ds(ks, tile), :]
            for m in maps:
                s_ref[hd, slot, m] = _dot(kt, q_ref[0, hd, m])

    def weighted_values(j, slot, maps=both):
        for hd in heads:
            vt = vt_ref[0, hd, j]
            for m in maps:
                acc_ref[hd, m] = alpha_ref[hd, slot, m] * acc_ref[hd, m] + _dot(vt, p_ref[hd, slot, m])

    def softmax_map(slot, near, m):
        for hd in heads:
            s = s_ref[hd, slot, m]
            if near is not None:
                s = s + bias_ref[hd, near]
            m_old = m_ref[hd, m]
            m_new = jnp.maximum(m_old, jnp.max(s, axis=0, keepdims=True))
            p = jnp.exp2(s - m_new)
            p_ref[hd, slot, m] = p.astype(BF16)
            alpha = jnp.exp2(m_old - m_new)
            alpha_ref[hd, slot, m] = alpha
            m_ref[hd, m] = m_new

    def step(r, slot, *, near=None, first=False, last=False):
        nxt = jnp.maximum(i - r - 1, 0)
        if first:
            scores(i, slot, maps=(1,))
        if not last:
            scores(nxt, 1 - slot, maps=(0,))
        if not first:
            weighted_values(i - r + 1, 1 - slot, maps=(0,))
        softmax_map(slot, near, 0)
        if not last:
            scores(nxt, 1 - slot, maps=(1,))
        if not first:
            weighted_values(i - r + 1, 1 - slot, maps=(1,))
        if last:
            weighted_values(i - r, slot, maps=(0,))
        softmax_map(slot, near, 1)
        if last:
            weighted_values(i - r, slot, maps=(1,))

    scores(i, 0, maps=(0,))
    for last in (False, True):
        pl.when((i == 0) == last)(functools.partial(step, 0, 0, near=0, first=True, last=last))
    for last in (False, True):
        pl.when(jnp.logical_and(i >= 1, (i == 1) == last))(functools.partial(step, 1, 1, near=1, last=last))

    def far_pair(c, carry):
        r = 2 * c + 2
        step(r, 0)
        step(r + 1, 1)
        return carry

    n_mid = jnp.maximum(i - 2, 0)
    lax.fori_loop(0, n_mid // 2, far_pair, 0)
    pl.when(jnp.logical_and(i >= 3, i % 2 == 1))(functools.partial(step, i - 1, 0))
    pl.when(jnp.logical_and(i >= 2, i % 2 == 0))(functools.partial(step, i, 0, last=True))
    pl.when(jnp.logical_and(i >= 3, i % 2 == 1))(functools.partial(step, i, 1, last=True))

    lam = _lam(lp_ref, lam0)
    for hd in heads:
        inv_l = [1.0 / acc_ref[hd, m, HD2:HD2 + 1, :] for m in both]
        ot = acc_ref[hd, 0, :HD2, :] * inv_l[0] - lam * (acc_ref[hd, 1, :HD2, :] * inv_l[1])
        scale = lax.rsqrt(jnp.mean(ot * ot, axis=0, keepdims=True) + EPS) * (1.0 - lam0)
        o_ref[0, hd] = ot * scale * dn_ref[...]


def _attn_prompt(lp, qt, kb, vtb, bias, dn_col, lam0):
    B, H, _, _, T = qt.shape
    tile = ATTN_TILE
    nh = 2 if H % 2 == 0 else 1
    return pl.pallas_call(
        functools.partial(_attn_prompt_kernel, tile=tile, lam0=lam0, nh=nh),
        grid=(B, H // nh, T // tile),
        in_specs=[_const_spec(lp.shape),
                  pl.BlockSpec((1, nh, 2, HD2, tile), lambda b, h, i: (b, h, 0, 0, i)),
                  pl.BlockSpec((1, nh, T, HD2), lambda b, h, i: (b, h, 0, 0)),
                  pl.BlockSpec((1, nh, T // tile, HD2 + ONES_ROWS, tile), lambda b, h, i: (b, h, 0, 0, 0)),
                  pl.BlockSpec((nh, 2, tile, tile), lambda b, h, i: (h, 0, 0, 0)),
                  _const_spec((HD2, 1))],
        out_specs=pl.BlockSpec((1, nh, HD2, tile), lambda b, h, i: (b, h, 0, i)),
        out_shape=jax.ShapeDtypeStruct((B, H, HD2, T), F32),
        scratch_shapes=[pltpu.VMEM((nh, 2, 2, tile, tile), F32),
                        pltpu.VMEM((nh, 2, 2, tile, tile), BF16),
                        pltpu.VMEM((nh, 2, 2, 1, tile), F32),
                        pltpu.VMEM((nh, 2, 1, tile), F32),
                        pltpu.VMEM((nh, 2, HD2 + ONES_ROWS, tile), F32)],
        compiler_params=_params(3),
        name="attn_prompt",
    )(lp, qt, kb, vtb, bias, dn_col)


def _attn_step_kernel(lp_ref, q_ref, kp_ref, vp_ref, kn_ref, vn_ref, bias_ref, dn_ref, o_ref, *, past, lam0):
    lam = _lam(lp_ref, lam0)
    dn = dn_ref[...]
    for hh in range(H_D):
        qm = _split_maps(q_ref[0, hh])
        kp = kp_ref[0, hh].astype(BF16)
        vp = vp_ref[0, hh].astype(BF16)
        kn = kn_ref[0, hh].astype(BF16)
        vn = vn_ref[0, hh].astype(BF16)
        bias_p = bias_ref[hh, 0, :, :past]
        bias_n = bias_ref[hh, 0, :, past:]
        res = []
        for m in range(2):
            s_p = _dot_nt(qm[m], kp) + bias_p
            s_n = _dot_nt(qm[m], kn) + bias_n
            mx = jnp.maximum(jnp.max(s_p, axis=-1, keepdims=True), jnp.max(s_n, axis=-1, keepdims=True))
            p_p = jnp.exp(s_p - mx)
            p_n = jnp.exp(s_n - mx)
            l = jnp.sum(p_p, axis=-1, keepdims=True) + jnp.sum(p_n, axis=-1, keepdims=True)
            acc = _dot(p_p.astype(BF16), vp) + _dot(p_n.astype(BF16), vn)
            res += [acc, l]
        o_ref[0, :, hh * HD2:(hh + 1) * HD2] = _attn_finish(res[0], res[1], res[2], res[3], lam, dn, lam0)


def _attn_step(lp, dq, k_past, v_past, dk, dv, bias, dn, lam0, layer):
    B, H, T, _ = dq.shape
    past = k_past.shape[3]
    qblk = pl.BlockSpec((1, H, T, HD2), lambda b: (b, 0, 0, 0))
    new = pl.BlockSpec((None, 1, H, T, HD2), lambda b: (layer, b, 0, 0, 0))
    old = pl.BlockSpec((None, 1, H, past, HD2), lambda b: (layer, b, 0, 0, 0))
    return pl.pallas_call(
        functools.partial(_attn_step_kernel, past=past, lam0=lam0),
        grid=(B,),
        in_specs=[_const_spec(lp.shape), qblk, old, old, new, new, _const_spec(bias.shape), _const_spec((1, HD2))],
        out_specs=pl.BlockSpec((1, T, H * HD2), lambda b: (b, 0, 0)),
        out_shape=jax.ShapeDtypeStruct((B, T, H * HD2), F32),
        compiler_params=_params(1),
        name="attn_step",
    )(lp, dq, k_past, v_past, dk, dv, bias, dn)


SUBLANES = 8
LANES = 128
HEADER = 2 * SUBLANES


def _interleave_pitch(nv):
    p = nv + SUBLANES
    return p if (p // SUBLANES) % 2 else p + SUBLANES


def _ffn_kernel(x_ref, og_ref, od_ref, wo_ref, lpost_ref, lpre_ref, lout_ref, wup_ref, cw_ref, cb_ref,
                cs_ref, wdn_ref, y_ref, cnew_ref, prev_ref, ext_ref, act_ref, tbuf_ref, pbuf_ref,
                *, n_seq, seq_rows, tiles_per_seq, d_ff, fc, od_transposed):
    ti = pl.program_id(0) % tiles_per_seq
    tm, D = x_ref.shape
    nv = seq_rows // SUBLANES
    pitch = _interleave_pitch(nv)
    n_slabs = D // LANES

    @pl.when(ti == 0)
    def _():
        prev_ref[...] = cs_ref[...]

    if od_transposed:
        od = jnp.concatenate([od_ref[0, hh].T for hh in range(H_D)], axis=1)
    else:
        od = od_ref[...]
    mix = _dot(og_ref[...].astype(BF16), wo_ref[:GV, :]) + _dot(od.astype(BF16), wo_ref[GV:, :])
    x1 = x_ref[...] + _rms(mix, lpost_ref[...])
    h = _rms(x1, lpre_ref[...])

    for l in range(n_slabs):
        for q in range(n_seq):
            for s in range(SUBLANES):
                t0 = q * seq_rows + nv * s
                tbuf_ref[l, pl.ds((q * SUBLANES + s) * pitch, nv), :] = h[t0:t0 + nv, l * LANES:(l + 1) * LANES]
    for l in range(n_slabs):
        for q in range(n_seq):
            for v in range(nv):
                r0 = q * seq_rows + SUBLANES * v
                pbuf_ref[r0:r0 + SUBLANES, l * LANES:(l + 1) * LANES] = \
                    tbuf_ref[l, pl.ds(q * SUBLANES * pitch + v, SUBLANES, stride=pitch), :]
    hp = pbuf_ref[...].astype(BF16)

    n_chunks = d_ff // fc
    split = (n_chunks + 1) // 2 + 1
    k_gelu = math.sqrt(2.0 / math.pi)
    first_row = lax.broadcasted_iota(jnp.int32, (SUBLANES, fc), 0) == 0

    def up_pair(c):
        return [_dot(hp, wup_ref[:, off:off + fc]) for off in (c * fc, d_ff + c * fc)]

    def header(block, carried):
        return jnp.where(first_row, carried, pltpu.roll(block, 1, 0))

    nxt = up_pair(0)
    for c in range(n_chunks):
        cur = nxt
        if c + 1 < n_chunks:
            nxt = up_pair(c + 1)
        halves = []
        for hf, off in enumerate((c * fc, d_ff + c * fc)):
            cols = slice(off, off + fc)
            buf = ext_ref.at[c % 2, hf]
            cw = cw_ref[:, cols]
            cb = cb_ref[:, cols]
            parts = []
            for q in range(n_seq):
                up = cur[hf][q * seq_rows:(q + 1) * seq_rows, :]
                base = q * (seq_rows + HEADER)
                buf[base:base + SUBLANES, :] = header(up[seq_rows - 2 * SUBLANES:seq_rows - SUBLANES, :],
                                                      prev_ref[q, 0:1, cols])
                buf[base + SUBLANES:base + HEADER, :] = header(up[seq_rows - SUBLANES:, :], prev_ref[q, 1:2, cols])
                buf[base + HEADER:base + HEADER + seq_rows, :] = up
                d2 = buf[base:base + seq_rows, :]
                d1 = buf[base + SUBLANES:base + SUBLANES + seq_rows, :]
                parts.append(cb + cw[0:1] * d2 + cw[1:2] * d1 + cw[2:3] * up)
                prev_ref[q, 0:1, cols] = up[seq_rows - SUBLANES - 1:seq_rows - SUBLANES, :]
                prev_ref[q, 1:2, cols] = up[seq_rows - 1:seq_rows, :]
            halves.append(parts[0] if n_seq == 1 else jnp.concatenate(parts, axis=0))
        g, u = halves
        gelu = g * (0.5 + 0.5 * jnp.tanh(g * (k_gelu + (k_gelu * 0.044715) * (g * g))))
        act_ref[:, c * fc:(c + 1) * fc] = (gelu * u).astype(BF16)
        if c + 1 == split:
            down = _dot(act_ref[:, :split * fc], wdn_ref[:split * fc, :])
    down = down + _dot(act_ref[:, split * fc:], wdn_ref[split * fc:, :])

    for l in range(n_slabs):
        tbuf_ref[l, pl.ds(0, tm), :] = down[:, l * LANES:(l + 1) * LANES]
    for l in range(n_slabs):
        for q in range(n_seq):
            for s in range(SUBLANES):
                for j in range(nv // SUBLANES):
                    t0 = q * seq_rows + nv * s + SUBLANES * j
                    pbuf_ref[t0:t0 + SUBLANES, l * LANES:(l + 1) * LANES] = \
                        tbuf_ref[l, pl.ds(q * seq_rows + SUBLANES * SUBLANES * j + s, SUBLANES, stride=SUBLANES), :]
    y_ref[...] = x1 + _rms(pbuf_ref[...], lout_ref[...])
    cnew_ref[...] = prev_ref[...]


def _ffn(x2d, og2d, od, B, T, w_out, lpost, lpre, lout, w_up, conv_w, conv_b, conv_state, w_down, layer):
    n_rows, D = x2d.shape
    f2 = w_up.shape[-1]
    d_ff = f2 // 2
    if T >= ROW_TILE:
        tm, n_seq, seq_rows, tps = ROW_TILE, 1, ROW_TILE, T // ROW_TILE
    else:
        tm, n_seq, seq_rows, tps = n_rows, B, T, 1
    row = lambda w: pl.BlockSpec((tm, w), lambda i: (i, 0))
    cs = pl.BlockSpec((n_seq, CONV_W - 1, f2), lambda i: (i // tps, 0, 0))
    od_transposed = od.ndim == 4
    if od_transposed:
        assert n_seq == 1
        od_spec = pl.BlockSpec((1, H_D, HD2, tm), lambda i: (i // tps, 0, 0, i % tps))
    else:
        od_spec = row(DA)
    fc = 256
    return pl.pallas_call(
        functools.partial(_ffn_kernel, n_seq=n_seq, seq_rows=seq_rows, tiles_per_seq=tps, d_ff=d_ff, fc=fc,
                          od_transposed=od_transposed),
        grid=(n_rows // tm,),
        in_specs=[row(D), row(GV), od_spec, _layer_weight_spec(w_out, layer), _const_spec((1, D)),
                  _const_spec((1, D)), _const_spec((1, D)), _layer_weight_spec(w_up, layer),
                  _const_spec(conv_w.shape), _const_spec((1, f2)), cs, _layer_weight_spec(w_down, layer)],
        out_specs=[row(D), cs],
        out_shape=[jax.ShapeDtypeStruct((n_rows, D), F32),
                   jax.ShapeDtypeStruct((B, CONV_W - 1, f2), F32)],
        scratch_shapes=[pltpu.VMEM((n_seq, CONV_W - 1, f2), F32),
                        pltpu.VMEM((2, 2, n_seq * (seq_rows + HEADER), fc), F32),
                        pltpu.VMEM((tm, d_ff), BF16),
                        pltpu.VMEM((D // LANES, max(tm, n_seq * SUBLANES * _interleave_pitch(seq_rows // SUBLANES)),
                                    LANES), F32),
                        pltpu.VMEM((tm, D), F32)],
        compiler_params=_params(1),
        name="ffn",
    )(x2d, og2d, od, w_out, lpost, lpre, lout, w_up, conv_w, conv_b, conv_state, w_down)


def _lambda_init(layer):
    return 0.8 - 0.6 * math.exp(-0.3 * layer)


def _layer(x2d, B, T, layer, depth, kv_slabs, past_k, past_v, gla_state, conv_state, bias, wts):
    (w_main, w_gd, w_gu, b_gu, gn, lp, dn, w_out, ln_pre, ln_post, lf_pre, lf_post,
     w_up, conv_w, conv_b, w_down) = wts
    prompt = past_k is None
    outs = _proj(x2d, B, T, ln_pre, w_main, w_gd, w_gu, b_gu, layer=layer, depth=depth, kv_slabs=kv_slabs,
                 emit_bf16=prompt)
    gq, gk, gv, gr, la, dk, dv = outs[:7]
    og, s_new = _gla(gq, gk, gv, gr, la, gla_state, gn, B, T)
    lam0 = _lambda_init(layer)
    if prompt:
        qt, kb, vtb = outs[7:]
        od = _attn_prompt(lp, qt, kb, vtb, bias, dn.reshape(-1, 1), lam0)
    else:
        od = _attn_step(lp, outs[7], past_k, past_v, dk, dv, bias, dn, lam0, layer).reshape(B * T, DA)
    y, c_new = _ffn(x2d, og.reshape(B * T, GV), od, B, T, w_out, ln_post, lf_pre, lf_post,
                    w_up, conv_w, conv_b, conv_state, w_down, layer)
    return y, (dk, dv), s_new, c_new


def kernel(x_prompt, x_sample, cache_k, cache_v, state_gla, state_conv, t5_table, w_in, w_gate_up, b_gate_up,
           gla_norm, lam_params, diff_norm, w_out, ln_mix_pre, ln_mix_post, ln_ffn_pre, ln_ffn_post,
           w_ffn_up, conv_w, conv_b, w_ffn_down):
    depth = w_in.shape[0]
    B, T, D = x_prompt.shape
    Bs, Ts, _ = x_sample.shape
    past = cache_k.shape[3]
    f2 = w_ffn_up.shape[2]
    gd0 = 2 * GQ + 2 * GV

    w_main = jnp.concatenate([w_in[:, :, :gd0], w_in[:, :, gd0 + GATE_RANK:]], axis=2).astype(BF16)
    w_gd = w_in[:, :, gd0:gd0 + GATE_RANK].astype(BF16)
    w_gu, w_out_b, w_up_b, w_down_b = (a.astype(BF16) for a in (w_gate_up, w_out, w_ffn_up, w_ffn_down))

    def layer_weights(l):
        row = lambda a: a[l].reshape(1, -1)
        return (w_main, w_gd, w_gu, row(b_gate_up), row(gla_norm), lam_params[l], row(diff_norm), w_out_b,
                row(ln_mix_pre), row(ln_mix_post), row(ln_ffn_pre), row(ln_ffn_post),
                w_up_b, conv_w[l], row(conv_b), w_down_b)

    wts = [layer_weights(l) for l in range(depth)]
    bias_prompt = _bias_tiles(t5_table, [0, ATTN_TILE], ATTN_TILE, ATTN_TILE, keys_on_rows=True, exp2_shifted=True)
    bias_step = _bias_tiles(t5_table, [past], Ts, past + Ts)

    y = x_prompt.reshape(B * T, D)
    kv_p, gp, cp = None, [], []
    zero_s = jnp.zeros((B, H_G, DK_G, DV_G), F32)
    zero_c = jnp.zeros((B, CONV_W - 1, f2), F32)
    for l in range(depth):
        y, kv_p, s_new, c_new = _layer(y, B, T, l, depth, kv_p, None, None, zero_s, zero_c, bias_prompt, wts[l])
        gp.append(s_new); cp.append(c_new)
    y_prompt = y.reshape(B, T, D)

    y = x_sample.reshape(Bs * Ts, D)
    kv_s, gs, cs = None, [], []
    for l in range(depth):
        y, kv_s, s_new, c_new = _layer(y, Bs, Ts, l, depth, kv_s, cache_k, cache_v, state_gla[l],
                                       state_conv[l], bias_step, wts[l])
        gs.append(s_new); cs.append(c_new)
    y_sample = y.reshape(Bs, Ts, D)

    return (y_prompt, y_sample, kv_p[0], kv_p[1], jnp.stack(gp), jnp.stack(cp),
            kv_s[0], kv_s[1], jnp.stack(gs), jnp.stack(cs))
```

```python
import functools
import math

import jax
import jax.numpy as jnp
from jax import lax
from jax.experimental import pallas as pl
from jax.experimental.pallas import tpu as pltpu

F32 = jnp.float32
BF16 = jnp.bfloat16

CHUNK = 64
H_G = 4
DK_G = 64
DV_G = 128
GATE_RANK = 16
GATE_NORMALIZER = 16.0
H_D = 4
DH_D = 64
NUM_BUCKETS = 32
MAX_DISTANCE = 128
CONV_W = 3
EPS = 1e-6

GQ = H_G * DK_G
GV = H_G * DV_G
DA = H_D * 2 * DH_D
HD2 = 2 * DH_D

VMEM_LIMIT_BYTES = 56 * 1024 * 1024
ROW_TILE = 512
GLA_ROWS = 256
ATTN_TILE = 512
ONES_ROWS = 16
T5_FAR_N = math.isqrt(2 ** 13 - 1) + 1
NEAR_KEYS = -(-(T5_FAR_N - 1) // 8) * 8
NEAR_QUERIES = -(-(T5_FAR_N - 1) // 128) * 128
NEG_INIT = -1e30
LOG2E = math.log2(math.e)


def _rms(x, w):
    return x * lax.rsqrt(jnp.mean(x * x, axis=-1, keepdims=True) + EPS) * w


def _dot(a, b):
    return jnp.dot(a, b, preferred_element_type=F32)


def _dot_nt(a, b):
    return lax.dot_general(a, b, (((1,), (1,)), ((), ())), preferred_element_type=F32)


def _params(n_grid):
    return pltpu.CompilerParams(dimension_semantics=("arbitrary",) * n_grid,
                                vmem_limit_bytes=VMEM_LIMIT_BYTES)


def _const_spec(shape):
    nd = len(shape)
    return pl.BlockSpec(shape, lambda *_: (0,) * nd, pipeline_mode=pl.Buffered(1))


def _layer_weight_spec(w, layer):
    return pl.BlockSpec((None,) + w.shape[1:], lambda *_: (layer, 0, 0), pipeline_mode=pl.Buffered(1))


def _bias_kernel(tab_ref, o_ref, *, starts, nq, nk, keys_on_rows, exp2_shifted):
    h = pl.program_id(0)
    nb = NUM_BUCKETS // 2
    max_exact = nb // 2
    shape = (nk, nq) if keys_on_rows else (nq, nk)
    q_axis, k_axis = (1, 0) if keys_on_rows else (0, 1)
    for d, (q0, k0) in enumerate(starts):
        qpos = q0 + lax.broadcasted_iota(jnp.int32, shape, q_axis)
        kpos = k0 + lax.broadcasted_iota(jnp.int32, shape, k_axis)
        rel = kpos - qpos
        ret = jnp.where(rel > 0, nb, 0)
        n = jnp.abs(rel)
        assert (nb, max_exact, MAX_DISTANCE) == (16, 8, 128)
        nn = jnp.maximum(n * n, 1)
        large = jnp.minimum(33 - lax.clz(nn), nb - 1)
        bucket = ret + jnp.where(n < max_exact, n, large)
        bias = jnp.zeros(shape, F32)
        for b in range(NUM_BUCKETS):
            bias = jnp.where(bucket == b, tab_ref[b, h], bias)
        if exp2_shifted:
            bias = (bias - tab_ref[nb - 1, h]) * LOG2E
        visible = (kpos // CHUNK) <= (qpos // CHUNK)
        o_ref[0, d] = jnp.where(visible, bias, -jnp.inf)


def _bias_tiles(t5_table, starts, nq, nk, *, keys_on_rows=False, exp2_shifted=False):
    nd = len(starts)
    shape = (nk, nq) if keys_on_rows else (nq, nk)
    return pl.pallas_call(
        functools.partial(_bias_kernel, starts=tuple(starts), nq=nq, nk=nk, keys_on_rows=keys_on_rows,
                          exp2_shifted=exp2_shifted),
        grid=(H_D,),
        in_specs=[pl.BlockSpec(memory_space=pltpu.SMEM)],
        out_specs=pl.BlockSpec((1, nd) + shape, lambda h: (h, 0, 0, 0)),
        out_shape=jax.ShapeDtypeStruct((H_D, nd) + shape, F32),
        compiler_params=_params(1),
        name="t5_bias_tiles",
    )(t5_table)


def _proj_kernel(*refs, n_seq, seq_rows, n_alias, prompt, layer, depth):
    x_ref, ln_ref, w_ref, wgd_ref, wgu_ref, bgu_ref = refs[:6]
    gq_ref, gk_ref, gv_ref, gr_ref, la_ref, dk_ref, dv_ref = refs[6 + n_alias:13 + n_alias]
    if n_alias == 0:
        for other in range(depth):
            if other != layer:
                dk_ref[other] = jnp.zeros(dk_ref.shape[1:], F32)
                dv_ref[other] = jnp.zeros(dv_ref.shape[1:], F32)
        dk_ref, dv_ref = dk_ref.at[layer], dv_ref.at[layer]
    if prompt:
        qt_ref, kb_ref, vt_ref = refs[13 + n_alias:]
    else:
        dq_ref, = refs[13 + n_alias:]
    h = _rms(x_ref[...], ln_ref[...]).astype(BF16)

    def seg(a, b):
        return _dot(h, w_ref[:, a:b])

    base = 2 * GQ + 2 * GV
    q = seg(base, base + DA) * DH_D ** -0.5
    k = seg(base + DA, base + 2 * DA)
    v = seg(base + 2 * DA, base + 3 * DA)
    gd = _dot(h, wgd_ref[...]).astype(BF16)
    z = _dot(gd, wgu_ref[...]) + bgu_ref[...]
    gq_ref[...] = seg(0, GQ) * DK_G ** -0.5
    gk_ref[...] = seg(GQ, 2 * GQ)
    gv_ref[...] = seg(2 * GQ, 2 * GQ + GV)
    gr_ref[...] = seg(2 * GQ + GV, 2 * GQ + 2 * GV)
    log_sig = jnp.minimum(z, 0.0) - jnp.log1p(jnp.exp(-jnp.abs(z)))
    la_ref[...] = log_sig / GATE_NORMALIZER
    feat = lax.broadcasted_iota(jnp.int32, (HD2, seq_rows), 0)
    for s in range(n_seq):
        rows = slice(s * seq_rows, (s + 1) * seq_rows)
        for hh in range(H_D):
            cols = slice(hh * HD2, (hh + 1) * HD2)
            dk_ref[s, hh] = k[rows, cols]
            dv_ref[s, hh] = v[rows, cols]
            if not prompt:
                dq_ref[s, hh] = q[rows, cols]
                continue
            qt = (q[rows, cols] * LOG2E).T
            qt_ref[s, hh, 0] = jnp.where(feat < DH_D, qt, 0.0).astype(BF16)
            qt_ref[s, hh, 1] = jnp.where(feat >= DH_D, qt, 0.0).astype(BF16)
            kb_ref[s, hh] = k[rows, cols].astype(BF16)
            for kb in range(seq_rows // ATTN_TILE):
                r0 = s * seq_rows + kb * ATTN_TILE
                vt_ref[s, hh, kb, :HD2] = v[r0:r0 + ATTN_TILE, cols].T.astype(BF16)
                vt_ref[s, hh, kb, HD2:] = jnp.ones((ONES_ROWS, ATTN_TILE), BF16)


def _proj(x2d, B, T, ln, w_main, w_gd, w_gu, b_gu, *, layer, depth, kv_slabs, emit_bf16):
    n_rows, D = x2d.shape
    if T >= ROW_TILE:
        tm, n_seq, seq_rows, tps = ROW_TILE, 1, ROW_TILE, T // ROW_TILE
    else:
        tm, n_seq, seq_rows, tps = n_rows, B, T, 1
    grid = (n_rows // tm,)
    row = lambda w: pl.BlockSpec((tm, w), lambda i: (i, 0))
    head = pl.BlockSpec((n_seq, H_D, seq_rows, HD2), lambda i: (i // tps, 0, i % tps, 0))
    if kv_slabs is None:
        slab = pl.BlockSpec((depth, n_seq, H_D, seq_rows, HD2), lambda i: (0, i // tps, 0, i % tps, 0))
    else:
        slab = pl.BlockSpec((None, n_seq, H_D, seq_rows, HD2), lambda i: (layer, i // tps, 0, i % tps, 0))
    f32_rows = lambda w: jax.ShapeDtypeStruct((n_rows, w), F32)
    head_shape = lambda dt: jax.ShapeDtypeStruct((B, H_D, T, HD2), dt)
    slab_shape = jax.ShapeDtypeStruct((depth, B, H_D, T, HD2), F32)
    out_specs = [row(GQ), row(GQ), row(GV), row(GV), row(GQ), slab, slab]
    out_shape = [f32_rows(GQ), f32_rows(GQ), f32_rows(GV), f32_rows(GV), f32_rows(GQ), slab_shape, slab_shape]
    if emit_bf16:
        kt = seq_rows // ATTN_TILE
        out_specs += [pl.BlockSpec((n_seq, H_D, 2, HD2, seq_rows), lambda i: (i // tps, 0, 0, 0, i % tps)),
                      head,
                      pl.BlockSpec((n_seq, H_D, kt, HD2 + ONES_ROWS, ATTN_TILE),
                                   lambda i: (i // tps, 0, i % tps, 0, 0))]
        out_shape += [jax.ShapeDtypeStruct((B, H_D, 2, HD2, T), BF16), head_shape(BF16),
                      jax.ShapeDtypeStruct((B, H_D, T // ATTN_TILE, HD2 + ONES_ROWS, ATTN_TILE), BF16)]
    else:
        out_specs += [head]
        out_shape += [head_shape(F32)]
    args = [x2d, ln, w_main, w_gd, w_gu, b_gu]
    in_specs = [row(D), _const_spec((1, D)), _layer_weight_spec(w_main, layer), _layer_weight_spec(w_gd, layer),
                _layer_weight_spec(w_gu, layer), _const_spec((1, GQ))]
    aliases = {}
    if kv_slabs is not None:
        aliases = {len(args): 5, len(args) + 1: 6}
        args += list(kv_slabs)
        in_specs += [pl.BlockSpec(memory_space=pl.ANY)] * 2
    return pl.pallas_call(
        functools.partial(_proj_kernel, n_seq=n_seq, seq_rows=seq_rows, n_alias=len(aliases), prompt=emit_bf16,
                          layer=layer, depth=depth),
        grid=grid,
        in_specs=in_specs,
        out_specs=out_specs,
        out_shape=out_shape,
        input_output_aliases=aliases,
        compiler_params=_params(1),
        name="proj",
    )(*args)


def _gla_kernel(q_ref, k_ref, v_ref, r_ref, la_ref, s0_ref, gn_ref, o_ref, sout_ref, s_ref, *, n_chunks, nb):
    t = pl.program_id(1)
    seqs = range(nb)

    @pl.when(t == 0)
    def _():
        for b in seqs:
            for hh in range(H_G):
                s_ref[b, hh * DK_G:(hh + 1) * DK_G, :] = s0_ref[b, hh]

    R = n_chunks * CHUNK
    ri = lax.broadcasted_iota(jnp.int32, (R, R), 0)
    ci = lax.broadcasted_iota(jnp.int32, (R, R), 1)
    causal = jnp.logical_and((ri // CHUNK) == (ci // CHUNK), ci <= ri)
    head_of_lane = lax.broadcasted_iota(jnp.int32, (R, GQ), 1) // DK_G
    tril = causal.astype(BF16)
    chunk_rows = [slice(c * CHUNK, (c + 1) * CHUNK) for c in range(n_chunks)]
    vcols = [slice(hh * DV_G, (hh + 1) * DV_G) for hh in range(H_G)]
    pad = [jnp.zeros((8 - n_chunks % 8, GQ), F32)] if n_chunks % 8 else []

    la = [la_ref[b] for b in seqs]
    a1 = [x.astype(BF16) for x in la]
    r1 = [x - y.astype(F32) for x, y in zip(la, a1)]
    a2 = [x.astype(BF16) for x in r1]
    a3 = [(x - y.astype(F32)).astype(BF16) for x, y in zip(r1, a2)]
    g = [_dot(tril, a1[b]) + _dot(tril, a2[b]) + _dot(tril, a3[b]) for b in seqs]
    last_rows = [[g[b][(c + 1) * CHUNK - 1:(c + 1) * CHUNK, :] for c in range(n_chunks)] for b in seqs]
    g_last = [jnp.concatenate([jnp.broadcast_to(x, (CHUNK, GQ)) for x in last_rows[b]], axis=0) for b in seqs]
    decay = [jnp.exp(jnp.concatenate(last_rows[b] + pad, axis=0).T) for b in seqs]
    v = [v_ref[b].astype(BF16) for b in seqs]
    q_dec = [q_ref[b] * jnp.exp(g[b]) for b in seqs]
    k_inv = [(k_ref[b] * jnp.exp(-g[b])).astype(BF16) for b in seqs]
    k_end = [k_ref[b] * jnp.exp(g_last[b] - g[b]) for b in seqs]
    qm = [[jnp.where(head_of_lane == hh, q_dec[b], 0.0).astype(BF16) for hh in range(H_G)] for b in seqs]

    o_inter = [[[] for _ in range(H_G)] for _ in seqs]
    state = [s_ref[b] for b in seqs]
    for c, rows in enumerate(chunk_rows):
        for b in seqs:
            s_b = state[b].astype(BF16)
            kt = k_end[b][rows].T.astype(BF16)
            upd = []
            for hh in range(H_G):
                o_inter[b][hh].append(_dot(qm[b][hh][rows], s_b))
                upd.append(_dot(kt[hh * DK_G:(hh + 1) * DK_G, :], v[b][rows, vcols[hh]]))
            state[b] = decay[b][:, c:c + 1] * state[b] + jnp.concatenate(upd, axis=0)
    for b in seqs:
        s_ref[b] = state[b]

    gn = gn_ref[...]
    for hh in range(H_G):
        for b in seqs:
            a = jnp.where(causal, _dot_nt(qm[b][hh], k_inv[b]), 0.0)
            o = _dot(a.astype(BF16), v[b][:, vcols[hh]]) + jnp.concatenate(o_inter[b][hh], axis=0)
            r = r_ref[b, :, vcols[hh]]
            o_ref[b, :, vcols[hh]] = _rms(o, gn) * (r * (1.0 / (1.0 + jnp.exp(-r))))

    for b in seqs:
        for hh in range(H_G):
            sout_ref[b, hh] = state[b][hh * DK_G:(hh + 1) * DK_G, :]


def _gla(gq, gk, gv, gr, la, s0, gn, B, T):
    tb = min(GLA_ROWS, T)
    nb = 4 if B % 4 == 0 else 1
    r3 = lambda a: a.reshape(B, T, a.shape[-1])
    blk = lambda w: pl.BlockSpec((nb, tb, w), lambda b, t: (b, t, 0))
    st = pl.BlockSpec((nb, H_G, DK_G, DV_G), lambda b, t: (b, 0, 0, 0))
    return pl.pallas_call(
        functools.partial(_gla_kernel, n_chunks=tb // CHUNK, nb=nb),
        grid=(B // nb, T // tb),
        in_specs=[blk(GQ), blk(GQ), blk(GV), blk(GV), blk(GQ), st, _const_spec((1, DV_G))],
        out_specs=[blk(GV), st],
        out_shape=[jax.ShapeDtypeStruct((B, T, GV), F32),
                   jax.ShapeDtypeStruct((B, H_G, DK_G, DV_G), F32)],
        scratch_shapes=[pltpu.VMEM((nb, GQ, DV_G), F32)],
        compiler_params=_params(2),
        name="gla",
    )(r3(gq), r3(gk), r3(gv), r3(gr), r3(la), s0, gn)


def _lam(lp_ref, lam0):
    lp = lp_ref[...]
    t1 = jnp.sum(lp[0:1] * lp[1:2], axis=-1, keepdims=True)
    t2 = jnp.sum(lp[2:3] * lp[3:4], axis=-1, keepdims=True)
    return jnp.exp(t1) - jnp.exp(t2) + lam0


def _split_maps(q):
    lane = lax.broadcasted_iota(jnp.int32, q.shape, 1)
    return (jnp.where(lane < DH_D, q, 0.0).astype(BF16),
            jnp.where(lane >= DH_D, q, 0.0).astype(BF16))


def _attn_finish(acc0, l0, acc1, l1, lam, dn, lam0):
    o = acc0 * (1.0 / l0) - lam * (acc1 * (1.0 / l1))
    return _rms(o, dn) * (1.0 - lam0)


def _attn_prompt_kernel(lp_ref, q_ref, k_ref, vt_ref, bias_ref, corner_ref, dn_ref, o_ref,
                        s_ref, p_ref, alpha_ref, m_ref, acc_ref, *, tile, lam0, nh):
    i = pl.program_id(2)
    m_ref[...] = jnp.full(m_ref.shape, NEG_INIT, F32)
    acc_ref[...] = jnp.zeros(acc_ref.shape, F32)
    both = (0, 1)
    heads = range(nh)

    def scores(j, slot, maps=both):
        ks = pl.multiple_of(j * tile, tile)
        for hd in heads:
            kt = k_ref[0, hd, pl.ds(ks, tile), :]
            for m in maps:
                s_ref[hd, slot, m] = _dot(kt, q_ref[0, hd, m])

    def weighted_values(j, slot, maps=both):
        for hd in heads:
            vt = vt_ref[0, hd, j]
            for m in maps:
                acc_ref[hd, m] = alpha_ref[hd, slot, m] * acc_ref[hd, m] + _dot(vt, p_ref[hd, slot, m])

    def softmax_diagonal(slot, m):
        half = tile // 2
        for hd in heads:
            s_top = s_ref[hd, slot, m, :half, :] + bias_ref[hd, 0, :half, :]
            s_low = s_ref[hd, slot, m, half:, half:] + bias_ref[hd, 0, half:, half:]
            m_old = m_ref[hd, m]
            m_new = jnp.maximum(m_old, jnp.max(s_top, axis=0, keepdims=True))
            m_new = jnp.concatenate(
                [m_new[:, :half], jnp.maximum(m_new[:, half:], jnp.max(s_low, axis=0, keepdims=True))], axis=1)
            p_ref[hd, slot, m, :half, :] = jnp.exp2(s_top - m_new).astype(BF16)
            p_ref[hd, slot, m, half:, half:] = jnp.exp2(s_low - m_new[:, half:]).astype(BF16)
            p_ref[hd, slot, m, half:, :half] = jnp.zeros((half, half), BF16)
            alpha_ref[hd, slot, m] = jnp.exp2(m_old - m_new)
            m_ref[hd, m] = m_new

    def softmax_map(slot, near, m):
        if near == 0:
            return softmax_diagonal(slot, m)
        for hd in heads:
            s = s_ref[hd, slot, m]
            if near == 1:
                r0, c1 = tile - NEAR_KEYS, NEAR_QUERIES
                corner = s[r0:, :c1] + corner_ref[hd, 0]
                s = jnp.concatenate([s[:r0], jnp.concatenate([corner, s[r0:, c1:]], axis=1)], axis=0)
            m_old = m_ref[hd, m]
            m_new = jnp.maximum(m_old, jnp.max(s, axis=0, keepdims=True))
            p = jnp.exp2(s - m_new)
            p_ref[hd, slot, m] = p.astype(BF16)
            alpha = jnp.exp2(m_old - m_new)
            alpha_ref[hd, slot, m] = alpha
            m_ref[hd, m] = m_new

    def step(r, slot, *, near=None, first=False, last=False):
        nxt = jnp.maximum(i - r - 1, 0)
        if first:
            scores(i, slot, maps=(1,))
        if not last:
            scores(nxt, 1 - slot, maps=(0,))
        if not first:
            weighted_values(i - r + 1, 1 - slot, maps=(0,))
        softmax_map(slot, near, 0)
        if not last:
            scores(nxt, 1 - slot, maps=(1,))
        if not first:
            weighted_values(i - r + 1, 1 - slot, maps=(1,))
        if last:
            weighted_values(i - r, slot, maps=(0,))
        softmax_map(slot, near, 1)
        if last:
            weighted_values(i - r, slot, maps=(1,))

    scores(i, 0, maps=(0,))
    for last in (False, True):
        pl.when((i == 0) == last)(functools.partial(step, 0, 0, near=0, first=True, last=last))
    for last in (False, True):
        pl.when(jnp.logical_and(i >= 1, (i == 1) == last))(functools.partial(step, 1, 1, near=1, last=last))

    def far_pair(c, carry):
        r = 2 * c + 2
        step(r, 0)
        step(r + 1, 1)
        return carry

    n_mid = jnp.maximum(i - 2, 0)
    lax.fori_loop(0, n_mid // 2, far_pair, 0)
    pl.when(jnp.logical_and(i >= 3, i % 2 == 1))(functools.partial(step, i - 1, 0))
    pl.when(jnp.logical_and(i >= 2, i % 2 == 0))(functools.partial(step, i, 0, last=True))
    pl.when(jnp.logical_and(i >= 3, i % 2 == 1))(functools.partial(step, i, 1, last=True))

    lam = _lam(lp_ref, lam0)
    for hd in heads:
        inv_l = [1.0 / acc_ref[hd, m, HD2:HD2 + 1, :] for m in both]
        ot = acc_ref[hd, 0, :HD2, :] * inv_l[0] - lam * (acc_ref[hd, 1, :HD2, :] * inv_l[1])
        scale = lax.rsqrt(jnp.mean(ot * ot, axis=0, keepdims=True) + EPS) * (1.0 - lam0)
        o_ref[0, hd] = ot * scale * dn_ref[...]


def _attn_prompt(lp, qt, kb, vtb, biases, dn_col, lam0):
    B, H, _, _, T = qt.shape
    tile = ATTN_TILE
    nh = 2 if H % 2 == 0 else 1
    bias, corner = biases
    return pl.pallas_call(
        functools.partial(_attn_prompt_kernel, tile=tile, lam0=lam0, nh=nh),
        grid=(B, H // nh, T // tile),
        in_specs=[_const_spec(lp.shape),
                  pl.BlockSpec((1, nh, 2, HD2, tile), lambda b, h, i: (b, h, 0, 0, i)),
                  pl.BlockSpec((1, nh, T, HD2), lambda b, h, i: (b, h, 0, 0)),
                  pl.BlockSpec((1, nh, T // tile, HD2 + ONES_ROWS, tile), lambda b, h, i: (b, h, 0, 0, 0)),
                  pl.BlockSpec((nh, 1, tile, tile), lambda b, h, i: (h, 0, 0, 0)),
                  pl.BlockSpec((nh, 1, NEAR_KEYS, NEAR_QUERIES), lambda b, h, i: (h, 0, 0, 0)),
                  _const_spec((HD2, 1))],
        out_specs=pl.BlockSpec((1, nh, HD2, tile), lambda b, h, i: (b, h, 0, i)),
        out_shape=jax.ShapeDtypeStruct((B, H, HD2, T), F32),
        scratch_shapes=[pltpu.VMEM((nh, 2, 2, tile, tile), F32),
                        pltpu.VMEM((nh, 2, 2, tile, tile), BF16),
                        pltpu.VMEM((nh, 2, 2, 1, tile), F32),
                        pltpu.VMEM((nh, 2, 1, tile), F32),
                        pltpu.VMEM((nh, 2, HD2 + ONES_ROWS, tile), F32)],
        compiler_params=_params(3),
        name="attn_prompt",
    )(lp, qt, kb, vtb, bias, corner, dn_col)


def _attn_step_kernel(lp_ref, q_ref, kp_ref, vp_ref, kn_ref, vn_ref, bias_ref, dn_ref, o_ref, *, past, lam0):
    lam = _lam(lp_ref, lam0)
    dn = dn_ref[...]
    for hh in range(H_D):
        qm = _split_maps(q_ref[0, hh])
        kp = kp_ref[0, hh].astype(BF16)
        vp = vp_ref[0, hh].astype(BF16)
        kn = kn_ref[0, hh].astype(BF16)
        vn = vn_ref[0, hh].astype(BF16)
        bias_p = bias_ref[hh, 0, :, :past]
        bias_n = bias_ref[hh, 0, :, past:]
        res = []
        for m in range(2):
            s_p = _dot_nt(qm[m], kp) + bias_p
            s_n = _dot_nt(qm[m], kn) + bias_n
            mx = jnp.maximum(jnp.max(s_p, axis=-1, keepdims=True), jnp.max(s_n, axis=-1, keepdims=True))
            p_p = jnp.exp(s_p - mx)
            p_n = jnp.exp(s_n - mx)
            l = jnp.sum(p_p, axis=-1, keepdims=True) + jnp.sum(p_n, axis=-1, keepdims=True)
            acc = _dot(p_p.astype(BF16), vp) + _dot(p_n.astype(BF16), vn)
            res += [acc, l]
        o_ref[0, :, hh * HD2:(hh + 1) * HD2] = _attn_finish(res[0], res[1], res[2], res[3], lam, dn, lam0)


def _attn_step(lp, dq, k_past, v_past, dk, dv, bias, dn, lam0, layer):
    B, H, T, _ = dq.shape
    past = k_past.shape[3]
    qblk = pl.BlockSpec((1, H, T, HD2), lambda b: (b, 0, 0, 0))
    new = pl.BlockSpec((None, 1, H, T, HD2), lambda b: (layer, b, 0, 0, 0))
    old = pl.BlockSpec((None, 1, H, past, HD2), lambda b: (layer, b, 0, 0, 0))
    return pl.pallas_call(
        functools.partial(_attn_step_kernel, past=past, lam0=lam0),
        grid=(B,),
        in_specs=[_const_spec(lp.shape), qblk, old, old, new, new, _const_spec(bias.shape), _const_spec((1, HD2))],
        out_specs=pl.BlockSpec((1, T, H * HD2), lambda b: (b, 0, 0)),
        out_shape=jax.ShapeDtypeStruct((B, T, H * HD2), F32),
        compiler_params=_params(1),
        name="attn_step",
    )(lp, dq, k_past, v_past, dk, dv, bias, dn)


SUBLANES = 8
LANES = 128
HEADER = 2 * SUBLANES


def _interleave_pitch(nv):
    p = nv + SUBLANES
    return p if (p // SUBLANES) % 2 else p + SUBLANES


def _ffn_kernel(x_ref, og_ref, od_ref, wo_ref, lpost_ref, lpre_ref, lout_ref, wup_ref, cw_ref, cb_ref,
                cs_ref, wdn_ref, y_ref, cnew_ref, prev_ref, ext_ref, act_ref, tbuf_ref, pbuf_ref,
                *, n_seq, seq_rows, tiles_per_seq, d_ff, fc, od_transposed):
    ti = pl.program_id(0) % tiles_per_seq
    tm, D = x_ref.shape
    nv = seq_rows // SUBLANES
    pitch = _interleave_pitch(nv)
    n_slabs = D // LANES

    @pl.when(ti == 0)
    def _():
        prev_ref[...] = cs_ref[...]

    if od_transposed:
        od = jnp.concatenate([od_ref[0, hh].T for hh in range(H_D)], axis=1)
    else:
        od = od_ref[...]
    mix = _dot(og_ref[...].astype(BF16), wo_ref[:GV, :]) + _dot(od.astype(BF16), wo_ref[GV:, :])
    x1 = x_ref[...] + _rms(mix, lpost_ref[...])
    h = _rms(x1, lpre_ref[...])

    for l in range(n_slabs):
        for q in range(n_seq):
            for s in range(SUBLANES):
                t0 = q * seq_rows + nv * s
                tbuf_ref[l, pl.ds((q * SUBLANES + s) * pitch, nv), :] = h[t0:t0 + nv, l * LANES:(l + 1) * LANES]
    for l in range(n_slabs):
        for q in range(n_seq):
            for v in range(nv):
                r0 = q * seq_rows + SUBLANES * v
                pbuf_ref[r0:r0 + SUBLANES, l * LANES:(l + 1) * LANES] = \
                    tbuf_ref[l, pl.ds(q * SUBLANES * pitch + v, SUBLANES, stride=pitch), :]
    hp = pbuf_ref[...].astype(BF16)

    n_chunks = d_ff // fc
    split = (n_chunks + 1) // 2 + 1
    k_gelu = math.sqrt(2.0 / math.pi)
    first_row = lax.broadcasted_iota(jnp.int32, (SUBLANES, fc), 0) == 0

    def up_pair(c):
        return [_dot(hp, wup_ref[:, off:off + fc]) for off in (c * fc, d_ff + c * fc)]

    def header(block, carried):
        return jnp.where(first_row, carried, pltpu.roll(block, 1, 0))

    nxt = up_pair(0)
    for c in range(n_chunks):
        cur = nxt
        if c + 1 < n_chunks:
            nxt = up_pair(c + 1)
        halves = []
        for hf, off in enumerate((c * fc, d_ff + c * fc)):
            cols = slice(off, off + fc)
            buf = ext_ref.at[c % 2, hf]
            cw = cw_ref[:, cols]
            cb = cb_ref[:, cols]
            parts = []
            for q in range(n_seq):
                up = cur[hf][q * seq_rows:(q + 1) * seq_rows, :]
                base = q * (seq_rows + HEADER)
                buf[base:base + SUBLANES, :] = header(up[seq_rows - 2 * SUBLANES:seq_rows - SUBLANES, :],
                                                      prev_ref[q, 0:1, cols])
                buf[base + SUBLANES:base + HEADER, :] = header(up[seq_rows - SUBLANES:, :], prev_ref[q, 1:2, cols])
                buf[base + HEADER:base + HEADER + seq_rows, :] = up
                d2 = buf[base:base + seq_rows, :]
                d1 = buf[base + SUBLANES:base + SUBLANES + seq_rows, :]
                parts.append(cb + cw[0:1] * d2 + cw[1:2] * d1 + cw[2:3] * up)
                prev_ref[q, 0:1, cols] = up[seq_rows - SUBLANES - 1:seq_rows - SUBLANES, :]
                prev_ref[q, 1:2, cols] = up[seq_rows - 1:seq_rows, :]
            halves.append(parts[0] if n_seq == 1 else jnp.concatenate(parts, axis=0))
        g, u = halves
        gelu = g * (0.5 + 0.5 * jnp.tanh(g * (k_gelu + (k_gelu * 0.044715) * (g * g))))
        act_ref[:, c * fc:(c + 1) * fc] = (gelu * u).astype(BF16)
        if c + 1 == split:
            down = _dot(act_ref[:, :split * fc], wdn_ref[:split * fc, :])
    down = down + _dot(act_ref[:, split * fc:], wdn_ref[split * fc:, :])

    for l in range(n_slabs):
        tbuf_ref[l, pl.ds(0, tm), :] = down[:, l * LANES:(l + 1) * LANES]
    for l in range(n_slabs):
        for q in range(n_seq):
            for s in range(SUBLANES):
                for j in range(nv // SUBLANES):
                    t0 = q * seq_rows + nv * s + SUBLANES * j
                    pbuf_ref[t0:t0 + SUBLANES, l * LANES:(l + 1) * LANES] = \
                        tbuf_ref[l, pl.ds(q * seq_rows + SUBLANES * SUBLANES * j + s, SUBLANES, stride=SUBLANES), :]
    y_ref[...] = x1 + _rms(pbuf_ref[...], lout_ref[...])
    cnew_ref[...] = prev_ref[...]


def _ffn(x2d, og2d, od, B, T, w_out, lpost, lpre, lout, w_up, conv_w, conv_b, conv_state, w_down, layer):
    n_rows, D = x2d.shape
    f2 = w_up.shape[-1]
    d_ff = f2 // 2
    if T >= ROW_TILE:
        tm, n_seq, seq_rows, tps = ROW_TILE, 1, ROW_TILE, T // ROW_TILE
    else:
        tm, n_seq, seq_rows, tps = n_rows, B, T, 1
    row = lambda w: pl.BlockSpec((tm, w), lambda i: (i, 0))
    cs = pl.BlockSpec((n_seq, CONV_W - 1, f2), lambda i: (i // tps, 0, 0))
    od_transposed = od.ndim == 4
    if od_transposed:
        assert n_seq == 1
        od_spec = pl.BlockSpec((1, H_D, HD2, tm), lambda i: (i // tps, 0, 0, i % tps))
    else:
        od_spec = row(DA)
    fc = 256
    return pl.pallas_call(
        functools.partial(_ffn_kernel, n_seq=n_seq, seq_rows=seq_rows, tiles_per_seq=tps, d_ff=d_ff, fc=fc,
                          od_transposed=od_transposed),
        grid=(n_rows // tm,),
        in_specs=[row(D), row(GV), od_spec, _layer_weight_spec(w_out, layer), _const_spec((1, D)),
                  _const_spec((1, D)), _const_spec((1, D)), _layer_weight_spec(w_up, layer),
                  _const_spec(conv_w.shape), _const_spec((1, f2)), cs, _layer_weight_spec(w_down, layer)],
        out_specs=[row(D), cs],
        out_shape=[jax.ShapeDtypeStruct((n_rows, D), F32),
                   jax.ShapeDtypeStruct((B, CONV_W - 1, f2), F32)],
        scratch_shapes=[pltpu.VMEM((n_seq, CONV_W - 1, f2), F32),
                        pltpu.VMEM((2, 2, n_seq * (seq_rows + HEADER), fc), F32),
                        pltpu.VMEM((tm, d_ff), BF16),
                        pltpu.VMEM((D // LANES, max(tm, n_seq * SUBLANES * _interleave_pitch(seq_rows // SUBLANES)),
                                    LANES), F32),
                        pltpu.VMEM((tm, D), F32)],
        compiler_params=_params(1),
        name="ffn",
    )(x2d, og2d, od, w_out, lpost, lpre, lout, w_up, conv_w, conv_b, conv_state, w_down)


def _lambda_init(layer):
    return 0.8 - 0.6 * math.exp(-0.3 * layer)


def _layer(x2d, B, T, layer, depth, kv_slabs, past_k, past_v, gla_state, conv_state, bias, wts):
    (w_main, w_gd, w_gu, b_gu, gn, lp, dn, w_out, ln_pre, ln_post, lf_pre, lf_post,
     w_up, conv_w, conv_b, w_down) = wts
    prompt = past_k is None
    outs = _proj(x2d, B, T, ln_pre, w_main, w_gd, w_gu, b_gu, layer=layer, depth=depth, kv_slabs=kv_slabs,
                 emit_bf16=prompt)
    gq, gk, gv, gr, la, dk, dv = outs[:7]
    og, s_new = _gla(gq, gk, gv, gr, la, gla_state, gn, B, T)
    lam0 = _lambda_init(layer)
    if prompt:
        qt, kb, vtb = outs[7:]
        od = _attn_prompt(lp, qt, kb, vtb, bias, dn.reshape(-1, 1), lam0)
    else:
        od = _attn_step(lp, outs[7], past_k, past_v, dk, dv, bias, dn, lam0, layer).reshape(B * T, DA)
    y, c_new = _ffn(x2d, og.reshape(B * T, GV), od, B, T, w_out, ln_post, lf_pre, lf_post,
                    w_up, conv_w, conv_b, conv_state, w_down, layer)
    return y, (dk, dv), s_new, c_new


def kernel(x_prompt, x_sample, cache_k, cache_v, state_gla, state_conv, t5_table, w_in, w_gate_up, b_gate_up,
           gla_norm, lam_params, diff_norm, w_out, ln_mix_pre, ln_mix_post, ln_ffn_pre, ln_ffn_post,
           w_ffn_up, conv_w, conv_b, w_ffn_down):
    depth = w_in.shape[0]
    B, T, D = x_prompt.shape
    Bs, Ts, _ = x_sample.shape
    past = cache_k.shape[3]
    f2 = w_ffn_up.shape[2]
    gd0 = 2 * GQ + 2 * GV

    w_main = jnp.concatenate([w_in[:, :, :gd0], w_in[:, :, gd0 + GATE_RANK:]], axis=2).astype(BF16)
    w_gd = w_in[:, :, gd0:gd0 + GATE_RANK].astype(BF16)
    w_gu, w_out_b, w_up_b, w_down_b = (a.astype(BF16) for a in (w_gate_up, w_out, w_ffn_up, w_ffn_down))

    def layer_weights(l):
        row = lambda a: a[l].reshape(1, -1)
        return (w_main, w_gd, w_gu, row(b_gate_up), row(gla_norm), lam_params[l], row(diff_norm), w_out_b,
                row(ln_mix_pre), row(ln_mix_post), row(ln_ffn_pre), row(ln_ffn_post),
                w_up_b, conv_w[l], row(conv_b), w_down_b)

    wts = [layer_weights(l) for l in range(depth)]
    bias_prompt = (
        _bias_tiles(t5_table, [(0, 0)], ATTN_TILE, ATTN_TILE, keys_on_rows=True, exp2_shifted=True),
        _bias_tiles(t5_table, [(ATTN_TILE, ATTN_TILE - NEAR_KEYS)], NEAR_QUERIES, NEAR_KEYS,
                    keys_on_rows=True, exp2_shifted=True))
    bias_step = _bias_tiles(t5_table, [(past, 0)], Ts, past + Ts)

    y = x_prompt.reshape(B * T, D)
    kv_p, gp, cp = None, [], []
    zero_s = jnp.zeros((B, H_G, DK_G, DV_G), F32)
    zero_c = jnp.zeros((B, CONV_W - 1, f2), F32)
    for l in range(depth):
        y, kv_p, s_new, c_new = _layer(y, B, T, l, depth, kv_p, None, None, zero_s, zero_c, bias_prompt, wts[l])
        gp.append(s_new); cp.append(c_new)
    y_prompt = y.reshape(B, T, D)

    y = x_sample.reshape(Bs * Ts, D)
    kv_s, gs, cs = None, [], []
    for l in range(depth):
        y, kv_s, s_new, c_new = _layer(y, Bs, Ts, l, depth, kv_s, cache_k, cache_v, state_gla[l],
                                       state_conv[l], bias_step, wts[l])
        gs.append(s_new); cs.append(c_new)
    y_sample = y.reshape(Bs, Ts, D)

    return (y_prompt, y_sample, kv_p[0], kv_p[1], jnp.stack(gp), jnp.stack(cp),
            kv_s[0], kv_s[1], jnp.stack(gs), jnp.stack(cs))
```

```python
import functools
import math

import jax
import jax.numpy as jnp
from jax import lax
from jax.experimental import pallas as pl
from jax.experimental.pallas import tpu as pltpu

F32 = jnp.float32
BF16 = jnp.bfloat16

CHUNK = 64
H_G = 4
DK_G = 64
DV_G = 128
GATE_RANK = 16
GATE_NORMALIZER = 16.0
H_D = 4
DH_D = 64
NUM_BUCKETS = 32
MAX_DISTANCE = 128
CONV_W = 3
EPS = 1e-6

GQ = H_G * DK_G
GV = H_G * DV_G
DA = H_D * 2 * DH_D
HD2 = 2 * DH_D

VMEM_LIMIT_BYTES = 56 * 1024 * 1024
ROW_TILE = 512
GLA_ROWS = 256
ATTN_TILE = 512
ONES_ROWS = 16
T5_FAR_N = math.isqrt(2 ** 13 - 1) + 1
NEAR_KEYS = -(-(T5_FAR_N - 1) // 8) * 8
NEAR_QUERIES = -(-(T5_FAR_N - 1) // 128) * 128
NEG_INIT = -1e30
LOG2E = math.log2(math.e)


def _rms(x, w):
    return x * lax.rsqrt(jnp.mean(x * x, axis=-1, keepdims=True) + EPS) * w


def _dot(a, b):
    return jnp.dot(a, b, preferred_element_type=F32)


def _dot_nt(a, b):
    return lax.dot_general(a, b, (((1,), (1,)), ((), ())), preferred_element_type=F32)


def _params(n_grid):
    return pltpu.CompilerParams(dimension_semantics=("arbitrary",) * n_grid,
                                vmem_limit_bytes=VMEM_LIMIT_BYTES)


def _const_spec(shape):
    nd = len(shape)
    return pl.BlockSpec(shape, lambda *_: (0,) * nd, pipeline_mode=pl.Buffered(1))


def _layer_weight_spec(w, layer):
    return pl.BlockSpec((None,) + w.shape[1:], lambda *_: (layer, 0, 0), pipeline_mode=pl.Buffered(1))


def _bias_kernel(tab_ref, o_ref, *, starts, nq, nk, keys_on_rows, exp2_shifted):
    h = pl.program_id(0)
    nb = NUM_BUCKETS // 2
    max_exact = nb // 2
    shape = (nk, nq) if keys_on_rows else (nq, nk)
    q_axis, k_axis = (1, 0) if keys_on_rows else (0, 1)
    for d, (q0, k0) in enumerate(starts):
        qpos = q0 + lax.broadcasted_iota(jnp.int32, shape, q_axis)
        kpos = k0 + lax.broadcasted_iota(jnp.int32, shape, k_axis)
        rel = kpos - qpos
        ret = jnp.where(rel > 0, nb, 0)
        n = jnp.abs(rel)
        assert (nb, max_exact, MAX_DISTANCE) == (16, 8, 128)
        nn = jnp.maximum(n * n, 1)
        large = jnp.minimum(33 - lax.clz(nn), nb - 1)
        bucket = ret + jnp.where(n < max_exact, n, large)
        bias = jnp.zeros(shape, F32)
        for b in range(NUM_BUCKETS):
            bias = jnp.where(bucket == b, tab_ref[b, h], bias)
        if exp2_shifted:
            bias = (bias - tab_ref[nb - 1, h]) * LOG2E
        visible = (kpos // CHUNK) <= (qpos // CHUNK)
        o_ref[0, d] = jnp.where(visible, bias, -jnp.inf)


def _bias_tiles(t5_table, starts, nq, nk, *, keys_on_rows=False, exp2_shifted=False):
    nd = len(starts)
    shape = (nk, nq) if keys_on_rows else (nq, nk)
    return pl.pallas_call(
        functools.partial(_bias_kernel, starts=tuple(starts), nq=nq, nk=nk, keys_on_rows=keys_on_rows,
                          exp2_shifted=exp2_shifted),
        grid=(H_D,),
        in_specs=[pl.BlockSpec(memory_space=pltpu.SMEM)],
        out_specs=pl.BlockSpec((1, nd) + shape, lambda h: (h, 0, 0, 0)),
        out_shape=jax.ShapeDtypeStruct((H_D, nd) + shape, F32),
        compiler_params=_params(1),
        name="t5_bias_tiles",
    )(t5_table)


def _proj_kernel(*refs, n_seq, seq_rows, n_alias, prompt, layer, depth):
    x_ref, ln_ref, w_ref, wgd_ref, wgu_ref, bgu_ref = refs[:6]
    gq_ref, gk_ref, gv_ref, gr_ref, la_ref, dk_ref, dv_ref = refs[6 + n_alias:13 + n_alias]
    if n_alias == 0:
        for other in range(depth):
            if other != layer:
                dk_ref[other] = jnp.zeros(dk_ref.shape[1:], F32)
                dv_ref[other] = jnp.zeros(dv_ref.shape[1:], F32)
        dk_ref, dv_ref = dk_ref.at[layer], dv_ref.at[layer]
    if prompt:
        qt_ref, kb_ref, vt_ref = refs[13 + n_alias:]
    else:
        dq_ref, = refs[13 + n_alias:]
    h = _rms(x_ref[...], ln_ref[...]).astype(BF16)

    def seg(a, b):
        return _dot(h, w_ref[:, a:b])

    base = 2 * GQ + 2 * GV
    q = seg(base, base + DA) * DH_D ** -0.5
    k = seg(base + DA, base + 2 * DA)
    v = seg(base + 2 * DA, base + 3 * DA)
    gd = _dot(h, wgd_ref[...]).astype(BF16)
    z = _dot(gd, wgu_ref[...]) + bgu_ref[...]
    gq_ref[...] = seg(0, GQ) * DK_G ** -0.5
    gk_ref[...] = seg(GQ, 2 * GQ)
    gv_ref[...] = seg(2 * GQ, 2 * GQ + GV)
    gr_ref[...] = seg(2 * GQ + GV, 2 * GQ + 2 * GV)
    log_sig = jnp.minimum(z, 0.0) - jnp.log1p(jnp.exp(-jnp.abs(z)))
    la_ref[...] = log_sig / GATE_NORMALIZER
    feat = lax.broadcasted_iota(jnp.int32, (HD2, seq_rows), 0)
    for s in range(n_seq):
        rows = slice(s * seq_rows, (s + 1) * seq_rows)
        for hh in range(H_D):
            cols = slice(hh * HD2, (hh + 1) * HD2)
            dk_ref[s, hh] = k[rows, cols]
            dv_ref[s, hh] = v[rows, cols]
            if not prompt:
                dq_ref[s, hh] = q[rows, cols]
                continue
            qt = (q[rows, cols] * LOG2E).T
            qt_ref[s, hh, 0] = jnp.where(feat < DH_D, qt, 0.0).astype(BF16)
            qt_ref[s, hh, 1] = jnp.where(feat >= DH_D, qt, 0.0).astype(BF16)
            kb_ref[s, hh] = k[rows, cols].astype(BF16)
            for kb in range(seq_rows // ATTN_TILE):
                r0 = s * seq_rows + kb * ATTN_TILE
                vt_ref[s, hh, kb, :HD2] = v[r0:r0 + ATTN_TILE, cols].T.astype(BF16)
                vt_ref[s, hh, kb, HD2:] = jnp.ones((ONES_ROWS, ATTN_TILE), BF16)


def _proj(x2d, B, T, ln, w_main, w_gd, w_gu, b_gu, *, layer, depth, kv_slabs, emit_bf16):
    n_rows, D = x2d.shape
    if T >= ROW_TILE:
        tm, n_seq, seq_rows, tps = ROW_TILE, 1, ROW_TILE, T // ROW_TILE
    else:
        tm, n_seq, seq_rows, tps = n_rows, B, T, 1
    grid = (n_rows // tm,)
    row = lambda w: pl.BlockSpec((tm, w), lambda i: (i, 0))
    head = pl.BlockSpec((n_seq, H_D, seq_rows, HD2), lambda i: (i // tps, 0, i % tps, 0))
    if kv_slabs is None:
        slab = pl.BlockSpec((depth, n_seq, H_D, seq_rows, HD2), lambda i: (0, i // tps, 0, i % tps, 0))
    else:
        slab = pl.BlockSpec((None, n_seq, H_D, seq_rows, HD2), lambda i: (layer, i // tps, 0, i % tps, 0))
    f32_rows = lambda w: jax.ShapeDtypeStruct((n_rows, w), F32)
    head_shape = lambda dt: jax.ShapeDtypeStruct((B, H_D, T, HD2), dt)
    slab_shape = jax.ShapeDtypeStruct((depth, B, H_D, T, HD2), F32)
    out_specs = [row(GQ), row(GQ), row(GV), row(GV), row(GQ), slab, slab]
    out_shape = [f32_rows(GQ), f32_rows(GQ), f32_rows(GV), f32_rows(GV), f32_rows(GQ), slab_shape, slab_shape]
    if emit_bf16:
        kt = seq_rows // ATTN_TILE
        out_specs += [pl.BlockSpec((n_seq, H_D, 2, HD2, seq_rows), lambda i: (i // tps, 0, 0, 0, i % tps)),
                      head,
                      pl.BlockSpec((n_seq, H_D, kt, HD2 + ONES_ROWS, ATTN_TILE),
                                   lambda i: (i // tps, 0, i % tps, 0, 0))]
        out_shape += [jax.ShapeDtypeStruct((B, H_D, 2, HD2, T), BF16), head_shape(BF16),
                      jax.ShapeDtypeStruct((B, H_D, T // ATTN_TILE, HD2 + ONES_ROWS, ATTN_TILE), BF16)]
    else:
        out_specs += [head]
        out_shape += [head_shape(F32)]
    args = [x2d, ln, w_main, w_gd, w_gu, b_gu]
    in_specs = [row(D), _const_spec((1, D)), _layer_weight_spec(w_main, layer), _layer_weight_spec(w_gd, layer),
                _layer_weight_spec(w_gu, layer), _const_spec((1, GQ))]
    aliases = {}
    if kv_slabs is not None:
        aliases = {len(args): 5, len(args) + 1: 6}
        args += list(kv_slabs)
        in_specs += [pl.BlockSpec(memory_space=pl.ANY)] * 2
    return pl.pallas_call(
        functools.partial(_proj_kernel, n_seq=n_seq, seq_rows=seq_rows, n_alias=len(aliases), prompt=emit_bf16,
                          layer=layer, depth=depth),
        grid=grid,
        in_specs=in_specs,
        out_specs=out_specs,
        out_shape=out_shape,
        input_output_aliases=aliases,
        compiler_params=_params(1),
        name="proj",
    )(*args)


def _gla_kernel(q_ref, k_ref, v_ref, r_ref, la_ref, s0_ref, gn_ref, o_ref, sout_ref, s_ref, *, n_chunks, nb):
    t = pl.program_id(1)
    seqs = range(nb)

    @pl.when(t == 0)
    def _():
        for b in seqs:
            for hh in range(H_G):
                s_ref[b, hh * DK_G:(hh + 1) * DK_G, :] = s0_ref[b, hh]

    R = n_chunks * CHUNK
    ri = lax.broadcasted_iota(jnp.int32, (R, R), 0)
    ci = lax.broadcasted_iota(jnp.int32, (R, R), 1)
    causal = jnp.logical_and((ri // CHUNK) == (ci // CHUNK), ci <= ri)
    head_of_lane = lax.broadcasted_iota(jnp.int32, (R, GQ), 1) // DK_G
    tril = causal.astype(BF16)
    chunk_rows = [slice(c * CHUNK, (c + 1) * CHUNK) for c in range(n_chunks)]
    vcols = [slice(hh * DV_G, (hh + 1) * DV_G) for hh in range(H_G)]
    pad = [jnp.zeros((8 - n_chunks % 8, GQ), F32)] if n_chunks % 8 else []

    la = [la_ref[b] for b in seqs]
    a1 = [x.astype(BF16) for x in la]
    r1 = [x - y.astype(F32) for x, y in zip(la, a1)]
    a2 = [x.astype(BF16) for x in r1]
    a3 = [(x - y.astype(F32)).astype(BF16) for x, y in zip(r1, a2)]
    g = [_dot(tril, a1[b]) + _dot(tril, a2[b]) + _dot(tril, a3[b]) for b in seqs]
    last_rows = [[g[b][(c + 1) * CHUNK - 1:(c + 1) * CHUNK, :] for c in range(n_chunks)] for b in seqs]
    g_last = [jnp.concatenate([jnp.broadcast_to(x, (CHUNK, GQ)) for x in last_rows[b]], axis=0) for b in seqs]
    decay = [jnp.exp(jnp.concatenate(last_rows[b] + pad, axis=0).T) for b in seqs]
    v = [v_ref[b].astype(BF16) for b in seqs]
    q_dec = [q_ref[b] * jnp.exp(g[b]) for b in seqs]
    k_inv = [(k_ref[b] * jnp.exp(-g[b])).astype(BF16) for b in seqs]
    k_end = [k_ref[b] * jnp.exp(g_last[b] - g[b]) for b in seqs]
    qm = [[jnp.where(head_of_lane == hh, q_dec[b], 0.0).astype(BF16) for hh in range(H_G)] for b in seqs]

    o_inter = [[[] for _ in range(H_G)] for _ in seqs]
    state = [s_ref[b] for b in seqs]
    for c, rows in enumerate(chunk_rows):
        for b in seqs:
            s_b = state[b].astype(BF16)
            kt = k_end[b][rows].T.astype(BF16)
            upd = []
            for hh in range(H_G):
                o_inter[b][hh].append(_dot(qm[b][hh][rows], s_b))
                upd.append(_dot(kt[hh * DK_G:(hh + 1) * DK_G, :], v[b][rows, vcols[hh]]))
            state[b] = decay[b][:, c:c + 1] * state[b] + jnp.concatenate(upd, axis=0)
    for b in seqs:
        s_ref[b] = state[b]

    gn = gn_ref[...]
    for hh in range(H_G):
        for b in seqs:
            a = jnp.where(causal, _dot_nt(qm[b][hh], k_inv[b]), 0.0)
            o = _dot(a.astype(BF16), v[b][:, vcols[hh]]) + jnp.concatenate(o_inter[b][hh], axis=0)
            r = r_ref[b, :, vcols[hh]]
            o_ref[b, :, vcols[hh]] = _rms(o, gn) * (r * (1.0 / (1.0 + jnp.exp(-r))))

    for b in seqs:
        for hh in range(H_G):
            sout_ref[b, hh] = state[b][hh * DK_G:(hh + 1) * DK_G, :]


def _gla(gq, gk, gv, gr, la, s0, gn, B, T):
    tb = min(GLA_ROWS, T)
    nb = 4 if B % 4 == 0 else 1
    r3 = lambda a: a.reshape(B, T, a.shape[-1])
    blk = lambda w: pl.BlockSpec((nb, tb, w), lambda b, t: (b, t, 0))
    st = pl.BlockSpec((nb, H_G, DK_G, DV_G), lambda b, t: (b, 0, 0, 0))
    return pl.pallas_call(
        functools.partial(_gla_kernel, n_chunks=tb // CHUNK, nb=nb),
        grid=(B // nb, T // tb),
        in_specs=[blk(GQ), blk(GQ), blk(GV), blk(GV), blk(GQ), st, _const_spec((1, DV_G))],
        out_specs=[blk(GV), st],
        out_shape=[jax.ShapeDtypeStruct((B, T, GV), F32),
                   jax.ShapeDtypeStruct((B, H_G, DK_G, DV_G), F32)],
        scratch_shapes=[pltpu.VMEM((nb, GQ, DV_G), F32)],
        compiler_params=_params(2),
        name="gla",
    )(r3(gq), r3(gk), r3(gv), r3(gr), r3(la), s0, gn)


def _lam(lp_ref, lam0):
    lp = lp_ref[...]
    t1 = jnp.sum(lp[0:1] * lp[1:2], axis=-1, keepdims=True)
    t2 = jnp.sum(lp[2:3] * lp[3:4], axis=-1, keepdims=True)
    return jnp.exp(t1) - jnp.exp(t2) + lam0


def _split_maps(q):
    lane = lax.broadcasted_iota(jnp.int32, q.shape, 1)
    return (jnp.where(lane < DH_D, q, 0.0).astype(BF16),
            jnp.where(lane >= DH_D, q, 0.0).astype(BF16))


def _attn_finish(acc0, l0, acc1, l1, lam, dn, lam0):
    o = acc0 * (1.0 / l0) - lam * (acc1 * (1.0 / l1))
    return _rms(o, dn) * (1.0 - lam0)


def _attn_prompt_kernel(lp_ref, q_ref, qn_ref, k_ref, vt_ref, bias_ref, corner_ref, dn_ref, o_ref,
                        s_ref, p_ref, alpha_ref, m_ref, acc_ref, *, tile, lam0, nh):
    i = pl.program_id(2)
    m_ref[...] = jnp.full(m_ref.shape, NEG_INIT, F32)
    acc_ref[...] = jnp.zeros(acc_ref.shape, F32)
    both = (0, 1)
    heads = range(nh)

    def scores(j, slot, maps=both, queries=q_ref):
        ks = pl.multiple_of(j * tile, tile)
        for hd in heads:
            kt = k_ref[0, hd, pl.ds(ks, tile), :]
            for m in maps:
                s_ref[hd, slot, m] = _dot(kt, queries[0, hd, m])

    def weighted_values(j, slot, maps=both):
        for hd in heads:
            vt = vt_ref[0, hd, j]
            for m in maps:
                acc_ref[hd, m] = alpha_ref[hd, slot, m] * acc_ref[hd, m] + _dot(vt, p_ref[hd, slot, m])

    def softmax_diagonal(slot, m):
        half = tile // 2
        for hd in heads:
            s_top = s_ref[hd, slot, m, :half, :] + bias_ref[hd, 0, :half, :]
            s_low = s_ref[hd, slot, m, half:, half:] + bias_ref[hd, 0, half:, half:]
            m_old = m_ref[hd, m]
            m_new = jnp.maximum(m_old, jnp.max(s_top, axis=0, keepdims=True))
            m_new = jnp.concatenate(
                [m_new[:, :half], jnp.maximum(m_new[:, half:], jnp.max(s_low, axis=0, keepdims=True))], axis=1)
            p_ref[hd, slot, m, :half, :] = jnp.exp2(s_top - m_new).astype(BF16)
            p_ref[hd, slot, m, half:, half:] = jnp.exp2(s_low - m_new[:, half:]).astype(BF16)
            p_ref[hd, slot, m, half:, :half] = jnp.zeros((half, half), BF16)
            alpha_ref[hd, slot, m] = jnp.exp2(m_old - m_new)
            m_ref[hd, m] = m_new

    def softmax_map(slot, near, m):
        if near == 0:
            return softmax_diagonal(slot, m)
        for hd in heads:
            s = s_ref[hd, slot, m]
            if near == 1:
                r0, c1 = tile - NEAR_KEYS, NEAR_QUERIES
                corner = s[r0:, :c1] + corner_ref[hd, 0]
                s = jnp.concatenate([s[:r0], jnp.concatenate([corner, s[r0:, c1:]], axis=1)], axis=0)
            m_old = m_ref[hd, m]
            m_new = jnp.maximum(m_old, jnp.max(s, axis=0, keepdims=True))
            p = jnp.exp2(s - m_new)
            p_ref[hd, slot, m] = p.astype(BF16)
            alpha = jnp.exp2(m_old - m_new)
            alpha_ref[hd, slot, m] = alpha
            m_ref[hd, m] = m_new

    def step(r, slot, *, near=None, first=False, last=False):
        nxt = jnp.maximum(i - r - 1, 0)
        if first:
            scores(i, slot, maps=(1,))
        if not last:
            scores(nxt, 1 - slot, maps=(0,))
        if not first:
            weighted_values(i - r + 1, 1 - slot, maps=(0,))
        softmax_map(slot, near, 0)
        if not last:
            scores(nxt, 1 - slot, maps=(1,))
        if not first:
            weighted_values(i - r + 1, 1 - slot, maps=(1,))
        if last:
            weighted_values(i - r, slot, maps=(0,))
        softmax_map(slot, near, 1)
        if last:
            weighted_values(i - r, slot, maps=(1,))

    pl.when(i == 0)(functools.partial(scores, i, 0, maps=(0,)))
    for last in (False, True):
        pl.when((i == 0) == last)(functools.partial(step, 0, 0, near=0, first=True, last=last))
    for last in (False, True):
        pl.when(jnp.logical_and(i >= 1, (i == 1) == last))(functools.partial(step, 1, 1, near=1, last=last))

    def far_pair(c, carry):
        r = 2 * c + 2
        step(r, 0)
        step(r + 1, 1)
        return carry

    n_mid = jnp.maximum(i - 2, 0)
    lax.fori_loop(0, n_mid // 2, far_pair, 0)
    pl.when(jnp.logical_and(i >= 3, i % 2 == 1))(functools.partial(step, i - 1, 0))
    pl.when(jnp.logical_and(i >= 2, i % 2 == 0))(functools.partial(step, i, 0, last=True))
    pl.when(jnp.logical_and(i >= 3, i % 2 == 1))(functools.partial(step, i, 1, last=True))

    scores(jnp.minimum(i + 1, pl.num_programs(2) - 1), 0, maps=(0,), queries=qn_ref)
    lam = _lam(lp_ref, lam0)
    for hd in heads:
        inv_l = [1.0 / acc_ref[hd, m, HD2:HD2 + 1, :] for m in both]
        ot = acc_ref[hd, 0, :HD2, :] * inv_l[0] - lam * (acc_ref[hd, 1, :HD2, :] * inv_l[1])
        scale = lax.rsqrt(jnp.mean(ot * ot, axis=0, keepdims=True) + EPS) * (1.0 - lam0)
        o_ref[0, hd] = ot * scale * dn_ref[...]


def _attn_prompt(lp, qt, kb, vtb, biases, dn_col, lam0):
    B, H, _, _, T = qt.shape
    tile = ATTN_TILE
    nh = 2 if H % 2 == 0 else 1
    bias, corner = biases
    return pl.pallas_call(
        functools.partial(_attn_prompt_kernel, tile=tile, lam0=lam0, nh=nh),
        grid=(B, H // nh, T // tile),
        in_specs=[_const_spec(lp.shape),
                  pl.BlockSpec((1, nh, 2, HD2, tile), lambda b, h, i: (b, h, 0, 0, i)),
                  pl.BlockSpec((1, nh, 2, HD2, tile), lambda b, h, i: (b, h, 0, 0, jnp.minimum(i + 1, T // tile - 1))),
                  pl.BlockSpec((1, nh, T, HD2), lambda b, h, i: (b, h, 0, 0)),
                  pl.BlockSpec((1, nh, T // tile, HD2 + ONES_ROWS, tile), lambda b, h, i: (b, h, 0, 0, 0)),
                  pl.BlockSpec((nh, 1, tile, tile), lambda b, h, i: (h, 0, 0, 0)),
                  pl.BlockSpec((nh, 1, NEAR_KEYS, NEAR_QUERIES), lambda b, h, i: (h, 0, 0, 0)),
                  _const_spec((HD2, 1))],
        out_specs=pl.BlockSpec((1, nh, HD2, tile), lambda b, h, i: (b, h, 0, i)),
        out_shape=jax.ShapeDtypeStruct((B, H, HD2, T), F32),
        scratch_shapes=[pltpu.VMEM((nh, 2, 2, tile, tile), F32),
                        pltpu.VMEM((nh, 2, 2, tile, tile), BF16),
                        pltpu.VMEM((nh, 2, 2, 1, tile), F32),
                        pltpu.VMEM((nh, 2, 1, tile), F32),
                        pltpu.VMEM((nh, 2, HD2 + ONES_ROWS, tile), F32)],
        compiler_params=_params(3),
        name="attn_prompt",
    )(lp, qt, qt, kb, vtb, bias, corner, dn_col)


def _attn_step_kernel(lp_ref, q_ref, kp_ref, vp_ref, kn_ref, vn_ref, bias_ref, dn_ref, o_ref, *, past, lam0):
    lam = _lam(lp_ref, lam0)
    dn = dn_ref[...]
    for hh in range(H_D):
        qm = jnp.concatenate(_split_maps(q_ref[0, hh]), axis=0)
        t = qm.shape[0] // 2
        kp = kp_ref[0, hh].astype(BF16)
        vp = vp_ref[0, hh].astype(BF16)
        kn = kn_ref[0, hh].astype(BF16)
        vn = vn_ref[0, hh].astype(BF16)
        bias_p = bias_ref[hh, 0, :, :past]
        bias_n = bias_ref[hh, 0, :, past:]
        s_p = _dot_nt(qm, kp) + jnp.concatenate([bias_p, bias_p], axis=0)
        s_n = _dot_nt(qm, kn) + jnp.concatenate([bias_n, bias_n], axis=0)
        mx = jnp.maximum(jnp.max(s_p, axis=-1, keepdims=True), jnp.max(s_n, axis=-1, keepdims=True))
        p_p = jnp.exp(s_p - mx)
        p_n = jnp.exp(s_n - mx)
        l = jnp.sum(p_p, axis=-1, keepdims=True) + jnp.sum(p_n, axis=-1, keepdims=True)
        acc = _dot(p_p.astype(BF16), vp) + _dot(p_n.astype(BF16), vn)
        o_ref[0, :, hh * HD2:(hh + 1) * HD2] = _attn_finish(acc[:t], l[:t], acc[t:], l[t:], lam, dn, lam0)


def _attn_step(lp, dq, k_past, v_past, dk, dv, bias, dn, lam0, layer):
    B, H, T, _ = dq.shape
    past = k_past.shape[3]
    qblk = pl.BlockSpec((1, H, T, HD2), lambda b: (b, 0, 0, 0))
    new = pl.BlockSpec((None, 1, H, T, HD2), lambda b: (layer, b, 0, 0, 0))
    old = pl.BlockSpec((None, 1, H, past, HD2), lambda b: (layer, b, 0, 0, 0))
    return pl.pallas_call(
        functools.partial(_attn_step_kernel, past=past, lam0=lam0),
        grid=(B,),
        in_specs=[_const_spec(lp.shape), qblk, old, old, new, new, _const_spec(bias.shape), _const_spec((1, HD2))],
        out_specs=pl.BlockSpec((1, T, H * HD2), lambda b: (b, 0, 0)),
        out_shape=jax.ShapeDtypeStruct((B, T, H * HD2), F32),
        compiler_params=_params(1),
        name="attn_step",
    )(lp, dq, k_past, v_past, dk, dv, bias, dn)


SUBLANES = 8
LANES = 128
HEADER = 2 * SUBLANES


def _interleave_pitch(nv):
    p = nv + SUBLANES
    return p if (p // SUBLANES) % 2 else p + SUBLANES


def _ffn_kernel(x_ref, og_ref, od_ref, wo_ref, lpost_ref, lpre_ref, lout_ref, wup_ref, cw_ref, cb_ref,
                cs_ref, wdn_ref, y_ref, cnew_ref, prev_ref, ext_ref, act_ref, tbuf_ref, pbuf_ref,
                *, n_seq, seq_rows, tiles_per_seq, d_ff, fc, od_transposed):
    ti = pl.program_id(0) % tiles_per_seq
    tm, D = x_ref.shape
    nv = seq_rows // SUBLANES
    pitch = _interleave_pitch(nv)
    n_slabs = D // LANES

    @pl.when(ti == 0)
    def _():
        prev_ref[...] = cs_ref[...]

    if od_transposed:
        od = jnp.concatenate([od_ref[0, hh].T for hh in range(H_D)], axis=1)
    else:
        od = od_ref[...]
    mix = _dot(og_ref[...].astype(BF16), wo_ref[:GV, :]) + _dot(od.astype(BF16), wo_ref[GV:, :])
    x1 = x_ref[...] + _rms(mix, lpost_ref[...])
    h = _rms(x1, lpre_ref[...])

    for l in range(n_slabs):
        for q in range(n_seq):
            for s in range(SUBLANES):
                t0 = q * seq_rows + nv * s
                tbuf_ref[l, pl.ds((q * SUBLANES + s) * pitch, nv), :] = h[t0:t0 + nv, l * LANES:(l + 1) * LANES]
    for l in range(n_slabs):
        for q in range(n_seq):
            for v in range(nv):
                r0 = q * seq_rows + SUBLANES * v
                pbuf_ref[r0:r0 + SUBLANES, l * LANES:(l + 1) * LANES] = \
                    tbuf_ref[l, pl.ds(q * SUBLANES * pitch + v, SUBLANES, stride=pitch), :]
    hp = pbuf_ref[...].astype(BF16)

    n_chunks = d_ff // fc
    split = (n_chunks + 1) // 2 + 1
    k_gelu = math.sqrt(2.0 / math.pi)
    first_row = lax.broadcasted_iota(jnp.int32, (SUBLANES, fc), 0) == 0

    def up_pair(c):
        return [_dot(hp, wup_ref[:, off:off + fc]) for off in (c * fc, d_ff + c * fc)]

    def header(block, carried):
        return jnp.where(first_row, carried, pltpu.roll(block, 1, 0))

    nxt = up_pair(0)
    for c in range(n_chunks):
        cur = nxt
        if c + 1 < n_chunks:
            nxt = up_pair(c + 1)
        halves = []
        for hf, off in enumerate((c * fc, d_ff + c * fc)):
            cols = slice(off, off + fc)
            buf = ext_ref.at[c % 2, hf]
            cw = cw_ref[:, cols]
            cb = cb_ref[:, cols]
            parts = []
            for q in range(n_seq):
                up = cur[hf][q * seq_rows:(q + 1) * seq_rows, :]
                base = q * (seq_rows + HEADER)
                buf[base:base + SUBLANES, :] = header(up[seq_rows - 2 * SUBLANES:seq_rows - SUBLANES, :],
                                                      prev_ref[q, 0:1, cols])
                buf[base + SUBLANES:base + HEADER, :] = header(up[seq_rows - SUBLANES:, :], prev_ref[q, 1:2, cols])
                buf[base + HEADER:base + HEADER + seq_rows, :] = up
                d2 = buf[base:base + seq_rows, :]
                d1 = buf[base + SUBLANES:base + SUBLANES + seq_rows, :]
                parts.append(cb + cw[0:1] * d2 + cw[1:2] * d1 + cw[2:3] * up)
                prev_ref[q, 0:1, cols] = up[seq_rows - SUBLANES - 1:seq_rows - SUBLANES, :]
                prev_ref[q, 1:2, cols] = up[seq_rows - 1:seq_rows, :]
            halves.append(parts[0] if n_seq == 1 else jnp.concatenate(parts, axis=0))
        g, u = halves
        gelu = g * (0.5 + 0.5 * jnp.tanh(g * (k_gelu + (k_gelu * 0.044715) * (g * g))))
        act_ref[:, c * fc:(c + 1) * fc] = (gelu * u).astype(BF16)
        if c + 1 == split:
            down = _dot(act_ref[:, :split * fc], wdn_ref[:split * fc, :])
    down = down + _dot(act_ref[:, split * fc:], wdn_ref[split * fc:, :])

    for l in range(n_slabs):
        tbuf_ref[l, pl.ds(0, tm), :] = down[:, l * LANES:(l + 1) * LANES]
    for l in range(n_slabs):
        for q in range(n_seq):
            for s in range(SUBLANES):
                for j in range(nv // SUBLANES):
                    t0 = q * seq_rows + nv * s + SUBLANES * j
                    pbuf_ref[t0:t0 + SUBLANES, l * LANES:(l + 1) * LANES] = \
                        tbuf_ref[l, pl.ds(q * seq_rows + SUBLANES * SUBLANES * j + s, SUBLANES, stride=SUBLANES), :]
    y_ref[...] = x1 + _rms(pbuf_ref[...], lout_ref[...])
    cnew_ref[...] = prev_ref[...]


def _ffn(x2d, og2d, od, B, T, w_out, lpost, lpre, lout, w_up, conv_w, conv_b, conv_state, w_down, layer):
    n_rows, D = x2d.shape
    f2 = w_up.shape[-1]
    d_ff = f2 // 2
    if T >= ROW_TILE:
        tm, n_seq, seq_rows, tps = ROW_TILE, 1, ROW_TILE, T // ROW_TILE
    else:
        tm, n_seq, seq_rows, tps = n_rows, B, T, 1
    row = lambda w: pl.BlockSpec((tm, w), lambda i: (i, 0))
    cs = pl.BlockSpec((n_seq, CONV_W - 1, f2), lambda i: (i // tps, 0, 0))
    od_transposed = od.ndim == 4
    if od_transposed:
        assert n_seq == 1
        od_spec = pl.BlockSpec((1, H_D, HD2, tm), lambda i: (i // tps, 0, 0, i % tps))
    else:
        od_spec = row(DA)
    fc = 256
    return pl.pallas_call(
        functools.partial(_ffn_kernel, n_seq=n_seq, seq_rows=seq_rows, tiles_per_seq=tps, d_ff=d_ff, fc=fc,
                          od_transposed=od_transposed),
        grid=(n_rows // tm,),
        in_specs=[row(D), row(GV), od_spec, _layer_weight_spec(w_out, layer), _const_spec((1, D)),
                  _const_spec((1, D)), _const_spec((1, D)), _layer_weight_spec(w_up, layer),
                  _const_spec(conv_w.shape), _const_spec((1, f2)), cs, _layer_weight_spec(w_down, layer)],
        out_specs=[row(D), cs],
        out_shape=[jax.ShapeDtypeStruct((n_rows, D), F32),
                   jax.ShapeDtypeStruct((B, CONV_W - 1, f2), F32)],
        scratch_shapes=[pltpu.VMEM((n_seq, CONV_W - 1, f2), F32),
                        pltpu.VMEM((2, 2, n_seq * (seq_rows + HEADER), fc), F32),
                        pltpu.VMEM((tm, d_ff), BF16),
                        pltpu.VMEM((D // LANES, max(tm, n_seq * SUBLANES * _interleave_pitch(seq_rows // SUBLANES)),
                                    LANES), F32),
                        pltpu.VMEM((tm, D), F32)],
        compiler_params=_params(1),
        name="ffn",
    )(x2d, og2d, od, w_out, lpost, lpre, lout, w_up, conv_w, conv_b, conv_state, w_down)


def _lambda_init(layer):
    return 0.8 - 0.6 * math.exp(-0.3 * layer)


def _layer(x2d, B, T, layer, depth, kv_slabs, past_k, past_v, gla_state, conv_state, bias, wts):
    (w_main, w_gd, w_gu, b_gu, gn, lp, dn, w_out, ln_pre, ln_post, lf_pre, lf_post,
     w_up, conv_w, conv_b, w_down) = wts
    prompt = past_k is None
    outs = _proj(x2d, B, T, ln_pre, w_main, w_gd, w_gu, b_gu, layer=layer, depth=depth, kv_slabs=kv_slabs,
                 emit_bf16=prompt)
    gq, gk, gv, gr, la, dk, dv = outs[:7]
    og, s_new = _gla(gq, gk, gv, gr, la, gla_state, gn, B, T)
    lam0 = _lambda_init(layer)
    if prompt:
        qt, kb, vtb = outs[7:]
        od = _attn_prompt(lp, qt, kb, vtb, bias, dn.reshape(-1, 1), lam0)
    else:
        od = _attn_step(lp, outs[7], past_k, past_v, dk, dv, bias, dn, lam0, layer).reshape(B * T, DA)
    y, c_new = _ffn(x2d, og.reshape(B * T, GV), od, B, T, w_out, ln_post, lf_pre, lf_post,
                    w_up, conv_w, conv_b, conv_state, w_down, layer)
    return y, (dk, dv), s_new, c_new


def kernel(x_prompt, x_sample, cache_k, cache_v, state_gla, state_conv, t5_table, w_in, w_gate_up, b_gate_up,
           gla_norm, lam_params, diff_norm, w_out, ln_mix_pre, ln_mix_post, ln_ffn_pre, ln_ffn_post,
           w_ffn_up, conv_w, conv_b, w_ffn_down):
    depth = w_in.shape[0]
    B, T, D = x_prompt.shape
    Bs, Ts, _ = x_sample.shape
    past = cache_k.shape[3]
    f2 = w_ffn_up.shape[2]
    gd0 = 2 * GQ + 2 * GV

    w_main = jnp.concatenate([w_in[:, :, :gd0], w_in[:, :, gd0 + GATE_RANK:]], axis=2).astype(BF16)
    w_gd = w_in[:, :, gd0:gd0 + GATE_RANK].astype(BF16)
    w_gu, w_out_b, w_up_b, w_down_b = (a.astype(BF16) for a in (w_gate_up, w_out, w_ffn_up, w_ffn_down))

    def layer_weights(l):
        row = lambda a: a[l].reshape(1, -1)
        return (w_main, w_gd, w_gu, row(b_gate_up), row(gla_norm), lam_params[l], row(diff_norm), w_out_b,
                row(ln_mix_pre), row(ln_mix_post), row(ln_ffn_pre), row(ln_ffn_post),
                w_up_b, conv_w[l], row(conv_b), w_down_b)

    wts = [layer_weights(l) for l in range(depth)]
    bias_prompt = (
        _bias_tiles(t5_table, [(0, 0)], ATTN_TILE, ATTN_TILE, keys_on_rows=True, exp2_shifted=True),
        _bias_tiles(t5_table, [(ATTN_TILE, ATTN_TILE - NEAR_KEYS)], NEAR_QUERIES, NEAR_KEYS,
                    keys_on_rows=True, exp2_shifted=True))
    bias_step = _bias_tiles(t5_table, [(past, 0)], Ts, past + Ts)

    y = x_prompt.reshape(B * T, D)
    kv_p, gp, cp = None, [], []
    zero_s = jnp.zeros((B, H_G, DK_G, DV_G), F32)
    zero_c = jnp.zeros((B, CONV_W - 1, f2), F32)
    for l in range(depth):
        y, kv_p, s_new, c_new = _layer(y, B, T, l, depth, kv_p, None, None, zero_s, zero_c, bias_prompt, wts[l])
        gp.append(s_new); cp.append(c_new)
    y_prompt = y.reshape(B, T, D)

    y = x_sample.reshape(Bs * Ts, D)
    kv_s, gs, cs = None, [], []
    for l in range(depth):
        y, kv_s, s_new, c_new = _layer(y, Bs, Ts, l, depth, kv_s, cache_k, cache_v, state_gla[l],
                                       state_conv[l], bias_step, wts[l])
        gs.append(s_new); cs.append(c_new)
    y_sample = y.reshape(Bs, Ts, D)

    return (y_prompt, y_sample, kv_p[0], kv_p[1], jnp.stack(gp), jnp.stack(cp),
            kv_s[0], kv_s[1], jnp.stack(gs), jnp.stack(cs))
```

```python
import functools
import math

import jax
import jax.numpy as jnp
from jax import lax
from jax.experimental import pallas as pl
from jax.experimental.pallas import tpu as pltpu

F32 = jnp.float32
BF16 = jnp.bfloat16

CHUNK = 64
H_G = 4
DK_G = 64
DV_G = 128
GATE_RANK = 16
GATE_NORMALIZER = 16.0
H_D = 4
DH_D = 64
NUM_BUCKETS = 32
MAX_DISTANCE = 128
CONV_W = 3
EPS = 1e-6

GQ = H_G * DK_G
GV = H_G * DV_G
DA = H_D * 2 * DH_D
HD2 = 2 * DH_D

VMEM_LIMIT_BYTES = 56 * 1024 * 1024
ROW_TILE = 512
GLA_ROWS = 256
ATTN_TILE = 512
ONES_ROWS = 16
T5_FAR_N = math.isqrt(2 ** 13 - 1) + 1
NEAR_KEYS = -(-(T5_FAR_N - 1) // 8) * 8
NEAR_QUERIES = -(-(T5_FAR_N - 1) // 128) * 128
NEG_INIT = -1e30
LOG2E = math.log2(math.e)


def _rms(x, w):
    return x * lax.rsqrt(jnp.mean(x * x, axis=-1, keepdims=True) + EPS) * w


def _dot(a, b):
    return jnp.dot(a, b, preferred_element_type=F32)


def _dot_nt(a, b):
    return lax.dot_general(a, b, (((1,), (1,)), ((), ())), preferred_element_type=F32)


def _params(n_grid):
    return pltpu.CompilerParams(dimension_semantics=("arbitrary",) * n_grid,
                                vmem_limit_bytes=VMEM_LIMIT_BYTES)


def _const_spec(shape):
    nd = len(shape)
    return pl.BlockSpec(shape, lambda *_: (0,) * nd, pipeline_mode=pl.Buffered(1))


def _layer_weight_spec(w, layer):
    return pl.BlockSpec((None,) + w.shape[1:], lambda *_: (layer, 0, 0), pipeline_mode=pl.Buffered(1))


def _bias_kernel(tab_ref, o_ref, *, starts, nq, nk, keys_on_rows, exp2_shifted):
    h = pl.program_id(0)
    nb = NUM_BUCKETS // 2
    max_exact = nb // 2
    shape = (nk, nq) if keys_on_rows else (nq, nk)
    q_axis, k_axis = (1, 0) if keys_on_rows else (0, 1)
    for d, (q0, k0) in enumerate(starts):
        qpos = q0 + lax.broadcasted_iota(jnp.int32, shape, q_axis)
        kpos = k0 + lax.broadcasted_iota(jnp.int32, shape, k_axis)
        rel = kpos - qpos
        ret = jnp.where(rel > 0, nb, 0)
        n = jnp.abs(rel)
        assert (nb, max_exact, MAX_DISTANCE) == (16, 8, 128)
        nn = jnp.maximum(n * n, 1)
        large = jnp.minimum(33 - lax.clz(nn), nb - 1)
        bucket = ret + jnp.where(n < max_exact, n, large)
        bias = jnp.zeros(shape, F32)
        for b in range(NUM_BUCKETS):
            bias = jnp.where(bucket == b, tab_ref[b, h], bias)
        if exp2_shifted:
            bias = (bias - tab_ref[nb - 1, h]) * LOG2E
        visible = (kpos // CHUNK) <= (qpos // CHUNK)
        o_ref[0, d] = jnp.where(visible, bias, -jnp.inf)


def _bias_tiles(t5_table, starts, nq, nk, *, keys_on_rows=False, exp2_shifted=False):
    nd = len(starts)
    shape = (nk, nq) if keys_on_rows else (nq, nk)
    return pl.pallas_call(
        functools.partial(_bias_kernel, starts=tuple(starts), nq=nq, nk=nk, keys_on_rows=keys_on_rows,
                          exp2_shifted=exp2_shifted),
        grid=(H_D,),
        in_specs=[pl.BlockSpec(memory_space=pltpu.SMEM)],
        out_specs=pl.BlockSpec((1, nd) + shape, lambda h: (h, 0, 0, 0)),
        out_shape=jax.ShapeDtypeStruct((H_D, nd) + shape, F32),
        compiler_params=_params(1),
        name="t5_bias_tiles",
    )(t5_table)


def _regroup_kernel(w_ref, main_ref, gate_ref, *, gate_at):
    main_ref[:, :gate_at] = w_ref[:, :gate_at].astype(BF16)
    main_ref[:, gate_at:] = w_ref[:, gate_at + GATE_RANK:].astype(BF16)
    gate_ref[...] = w_ref[:, gate_at:gate_at + GATE_RANK].astype(BF16)


def _regroup_w_in(w_in, gate_at, rows_per_step=256):
    depth, D, cols = w_in.shape
    rows = depth * D
    main, gate = pl.pallas_call(
        functools.partial(_regroup_kernel, gate_at=gate_at),
        grid=(rows // rows_per_step,),
        in_specs=[pl.BlockSpec((rows_per_step, cols), lambda i: (i, 0))],
        out_specs=[pl.BlockSpec((rows_per_step, cols - GATE_RANK), lambda i: (i, 0)),
                   pl.BlockSpec((rows_per_step, GATE_RANK), lambda i: (i, 0))],
        out_shape=[jax.ShapeDtypeStruct((rows, cols - GATE_RANK), BF16),
                   jax.ShapeDtypeStruct((rows, GATE_RANK), BF16)],
        compiler_params=_params(1),
        name="regroup_w_in",
    )(w_in.reshape(rows, cols))
    return main.reshape(depth, D, cols - GATE_RANK), gate.reshape(depth, D, GATE_RANK)


def _proj_kernel(*refs, n_seq, seq_rows, n_alias, n_cast, prompt, layer, depth):
    x_ref, ln_ref, w_ref, wgd_ref, wgu_ref, bgu_ref = refs[:6]
    for src, dst in zip(refs[6 + n_alias:6 + n_alias + n_cast], refs[len(refs) - n_cast:]):
        dst[...] = src[...].astype(BF16)
    refs = refs[:6 + n_alias] + refs[6 + n_alias + n_cast:len(refs) - n_cast]
    gq_ref, gk_ref, gv_ref, gr_ref, la_ref, dk_ref, dv_ref = refs[6 + n_alias:13 + n_alias]
    if n_alias == 0:
        for other in range(depth):
            if other != layer:
                dk_ref[other] = jnp.zeros(dk_ref.shape[1:], F32)
                dv_ref[other] = jnp.zeros(dv_ref.shape[1:], F32)
        dk_ref, dv_ref = dk_ref.at[layer], dv_ref.at[layer]
    if prompt:
        qt_ref, kb_ref, vt_ref = refs[13 + n_alias:]
    else:
        dq_ref, = refs[13 + n_alias:]
    h = _rms(x_ref[...], ln_ref[...]).astype(BF16)

    def seg(a, b):
        return _dot(h, w_ref[:, a:b])

    base = 2 * GQ + 2 * GV
    q = seg(base, base + DA) * DH_D ** -0.5
    k = seg(base + DA, base + 2 * DA)
    v = seg(base + 2 * DA, base + 3 * DA)
    gd = _dot(h, wgd_ref[...]).astype(BF16)
    z = _dot(gd, wgu_ref[...]) + bgu_ref[...]
    gq_ref[...] = seg(0, GQ) * DK_G ** -0.5
    gk_ref[...] = seg(GQ, 2 * GQ)
    gv_ref[...] = seg(2 * GQ, 2 * GQ + GV)
    gr_ref[...] = seg(2 * GQ + GV, 2 * GQ + 2 * GV)
    log_sig = jnp.minimum(z, 0.0) - jnp.log1p(jnp.exp(-jnp.abs(z)))
    la_ref[...] = log_sig / GATE_NORMALIZER
    feat = lax.broadcasted_iota(jnp.int32, (HD2, seq_rows), 0)
    for s in range(n_seq):
        rows = slice(s * seq_rows, (s + 1) * seq_rows)
        for hh in range(H_D):
            cols = slice(hh * HD2, (hh + 1) * HD2)
            dk_ref[s, hh] = k[rows, cols]
            dv_ref[s, hh] = v[rows, cols]
            if not prompt:
                dq_ref[s, hh] = q[rows, cols]
                continue
            qt = (q[rows, cols] * LOG2E).T
            qt_ref[s, hh, 0] = jnp.where(feat < DH_D, qt, 0.0).astype(BF16)
            qt_ref[s, hh, 1] = jnp.where(feat >= DH_D, qt, 0.0).astype(BF16)
            kb_ref[s, hh] = k[rows, cols].astype(BF16)
            for kb in range(seq_rows // ATTN_TILE):
                r0 = s * seq_rows + kb * ATTN_TILE
                vt_ref[s, hh, kb, :HD2] = v[r0:r0 + ATTN_TILE, cols].T.astype(BF16)
                vt_ref[s, hh, kb, HD2:] = jnp.ones((ONES_ROWS, ATTN_TILE), BF16)


def _proj(x2d, B, T, ln, w_main, w_gd, w_gu, b_gu, *, layer, depth, kv_slabs, emit_bf16, cast=()):
    n_rows, D = x2d.shape
    if T >= ROW_TILE:
        tm, n_seq, seq_rows, tps = ROW_TILE, 1, ROW_TILE, T // ROW_TILE
    else:
        tm, n_seq, seq_rows, tps = n_rows, B, T, 1
    grid = (n_rows // tm,)
    row = lambda w: pl.BlockSpec((tm, w), lambda i: (i, 0))
    head = pl.BlockSpec((n_seq, H_D, seq_rows, HD2), lambda i: (i // tps, 0, i % tps, 0))
    if kv_slabs is None:
        slab = pl.BlockSpec((depth, n_seq, H_D, seq_rows, HD2), lambda i: (0, i // tps, 0, i % tps, 0))
    else:
        slab = pl.BlockSpec((None, n_seq, H_D, seq_rows, HD2), lambda i: (layer, i // tps, 0, i % tps, 0))
    f32_rows = lambda w: jax.ShapeDtypeStruct((n_rows, w), F32)
    head_shape = lambda dt: jax.ShapeDtypeStruct((B, H_D, T, HD2), dt)
    slab_shape = jax.ShapeDtypeStruct((depth, B, H_D, T, HD2), F32)
    out_specs = [row(GQ), row(GQ), row(GV), row(GV), row(GQ), slab, slab]
    out_shape = [f32_rows(GQ), f32_rows(GQ), f32_rows(GV), f32_rows(GV), f32_rows(GQ), slab_shape, slab_shape]
    if emit_bf16:
        kt = seq_rows // ATTN_TILE
        out_specs += [pl.BlockSpec((n_seq, H_D, 2, HD2, seq_rows), lambda i: (i // tps, 0, 0, 0, i % tps)),
                      head,
                      pl.BlockSpec((n_seq, H_D, kt, HD2 + ONES_ROWS, ATTN_TILE),
                                   lambda i: (i // tps, 0, i % tps, 0, 0))]
        out_shape += [jax.ShapeDtypeStruct((B, H_D, 2, HD2, T), BF16), head_shape(BF16),
                      jax.ShapeDtypeStruct((B, H_D, T // ATTN_TILE, HD2 + ONES_ROWS, ATTN_TILE), BF16)]
    else:
        out_specs += [head]
        out_shape += [head_shape(F32)]
    args = [x2d, ln, w_main, w_gd, w_gu, b_gu]
    in_specs = [row(D), _const_spec((1, D)), _layer_weight_spec(w_main, layer), _layer_weight_spec(w_gd, layer),
                _layer_weight_spec(w_gu, layer), _const_spec((1, GQ))]
    aliases = {}
    if kv_slabs is not None:
        aliases = {len(args): 5, len(args) + 1: 6}
        args += list(kv_slabs)
        in_specs += [pl.BlockSpec(memory_space=pl.ANY)] * 2
    for w in cast:
        rows, cols = w.shape
        assert rows % (16 * grid[0]) == 0, (rows, grid)
        blk = pl.BlockSpec((rows // grid[0], cols), lambda i: (i, 0))
        args.append(w)
        in_specs.append(blk)
        out_specs.append(blk)
        out_shape.append(jax.ShapeDtypeStruct(w.shape, BF16))
    return pl.pallas_call(
        functools.partial(_proj_kernel, n_seq=n_seq, seq_rows=seq_rows, n_alias=len(aliases), n_cast=len(cast),
                          prompt=emit_bf16, layer=layer, depth=depth),
        grid=grid,
        in_specs=in_specs,
        out_specs=out_specs,
        out_shape=out_shape,
        input_output_aliases=aliases,
        compiler_params=_params(1),
        name="proj",
    )(*args)


def _gla_kernel(q_ref, k_ref, v_ref, r_ref, la_ref, s0_ref, gn_ref, o_ref, sout_ref, s_ref, *, n_chunks, nb):
    t = pl.program_id(1)
    seqs = range(nb)

    @pl.when(t == 0)
    def _():
        for b in seqs:
            for hh in range(H_G):
                s_ref[b, hh * DK_G:(hh + 1) * DK_G, :] = s0_ref[b, hh]

    R = n_chunks * CHUNK
    ri = lax.broadcasted_iota(jnp.int32, (R, R), 0)
    ci = lax.broadcasted_iota(jnp.int32, (R, R), 1)
    causal = jnp.logical_and((ri // CHUNK) == (ci // CHUNK), ci <= ri)
    head_of_lane = lax.broadcasted_iota(jnp.int32, (R, GQ), 1) // DK_G
    tril = causal.astype(BF16)
    chunk_rows = [slice(c * CHUNK, (c + 1) * CHUNK) for c in range(n_chunks)]
    vcols = [slice(hh * DV_G, (hh + 1) * DV_G) for hh in range(H_G)]
    pad = [jnp.zeros((8 - n_chunks % 8, GQ), F32)] if n_chunks % 8 else []

    la = [la_ref[b] for b in seqs]
    a1 = [x.astype(BF16) for x in la]
    r1 = [x - y.astype(F32) for x, y in zip(la, a1)]
    a2 = [x.astype(BF16) for x in r1]
    a3 = [(x - y.astype(F32)).astype(BF16) for x, y in zip(r1, a2)]
    g = [_dot(tril, a1[b]) + _dot(tril, a2[b]) + _dot(tril, a3[b]) for b in seqs]
    last_rows = [[g[b][(c + 1) * CHUNK - 1:(c + 1) * CHUNK, :] for c in range(n_chunks)] for b in seqs]
    g_last = [jnp.concatenate([jnp.broadcast_to(x, (CHUNK, GQ)) for x in last_rows[b]], axis=0) for b in seqs]
    decay = [jnp.exp(jnp.concatenate(last_rows[b] + pad, axis=0).T) for b in seqs]
    v = [v_ref[b].astype(BF16) for b in seqs]
    q_dec = [q_ref[b] * jnp.exp(g[b]) for b in seqs]
    k_inv = [(k_ref[b] * jnp.exp(-g[b])).astype(BF16) for b in seqs]
    k_end = [k_ref[b] * jnp.exp(g_last[b] - g[b]) for b in seqs]
    qm = [[jnp.where(head_of_lane == hh, q_dec[b], 0.0).astype(BF16) for hh in range(H_G)] for b in seqs]

    o_inter = [[[] for _ in range(H_G)] for _ in seqs]
    state = [s_ref[b] for b in seqs]
    for c, rows in enumerate(chunk_rows):
        for b in seqs:
            s_b = state[b].astype(BF16)
            kt = k_end[b][rows].T.astype(BF16)
            upd = []
            for hh in range(H_G):
                o_inter[b][hh].append(_dot(qm[b][hh][rows], s_b))
                upd.append(_dot(kt[hh * DK_G:(hh + 1) * DK_G, :], v[b][rows, vcols[hh]]))
            state[b] = decay[b][:, c:c + 1] * state[b] + jnp.concatenate(upd, axis=0)
    for b in seqs:
        s_ref[b] = state[b]

    gn = gn_ref[...]
    for hh in range(H_G):
        for b in seqs:
            a = jnp.where(causal, _dot_nt(qm[b][hh], k_inv[b]), 0.0)
            o = _dot(a.astype(BF16), v[b][:, vcols[hh]]) + jnp.concatenate(o_inter[b][hh], axis=0)
            r = r_ref[b, :, vcols[hh]]
            o_ref[b, :, vcols[hh]] = _rms(o, gn) * (r * (1.0 / (1.0 + jnp.exp(-r))))

    for b in seqs:
        for hh in range(H_G):
            sout_ref[b, hh] = state[b][hh * DK_G:(hh + 1) * DK_G, :]


def _gla(gq, gk, gv, gr, la, s0, gn, B, T):
    tb = min(GLA_ROWS, T)
    nb = 4 if B % 4 == 0 else 1
    r3 = lambda a: a.reshape(B, T, a.shape[-1])
    blk = lambda w: pl.BlockSpec((nb, tb, w), lambda b, t: (b, t, 0))
    st = pl.BlockSpec((nb, H_G, DK_G, DV_G), lambda b, t: (b, 0, 0, 0))
    return pl.pallas_call(
        functools.partial(_gla_kernel, n_chunks=tb // CHUNK, nb=nb),
        grid=(B // nb, T // tb),
        in_specs=[blk(GQ), blk(GQ), blk(GV), blk(GV), blk(GQ), st, _const_spec((1, DV_G))],
        out_specs=[blk(GV), st],
        out_shape=[jax.ShapeDtypeStruct((B, T, GV), F32),
                   jax.ShapeDtypeStruct((B, H_G, DK_G, DV_G), F32)],
        scratch_shapes=[pltpu.VMEM((nb, GQ, DV_G), F32)],
        compiler_params=_params(2),
        name="gla",
    )(r3(gq), r3(gk), r3(gv), r3(gr), r3(la), s0, gn)


def _lam(lp_ref, lam0):
    lp = lp_ref[...]
    t1 = jnp.sum(lp[0:1] * lp[1:2], axis=-1, keepdims=True)
    t2 = jnp.sum(lp[2:3] * lp[3:4], axis=-1, keepdims=True)
    return jnp.exp(t1) - jnp.exp(t2) + lam0


def _split_maps(q):
    lane = lax.broadcasted_iota(jnp.int32, q.shape, 1)
    return (jnp.where(lane < DH_D, q, 0.0).astype(BF16),
            jnp.where(lane >= DH_D, q, 0.0).astype(BF16))


def _attn_finish(acc0, l0, acc1, l1, lam, dn, lam0):
    o = acc0 * (1.0 / l0) - lam * (acc1 * (1.0 / l1))
    return _rms(o, dn) * (1.0 - lam0)


def _attn_prompt_kernel(lp_ref, q_ref, qn_ref, k_ref, vt_ref, bias_ref, corner_ref, dn_ref, o_ref,
                        s_ref, p_ref, alpha_ref, m_ref, acc_ref, *, tile, lam0, nh):
    i = pl.program_id(2)
    m_ref[...] = jnp.full(m_ref.shape, NEG_INIT, F32)
    acc_ref[...] = jnp.zeros(acc_ref.shape, F32)
    both = (0, 1)
    heads = range(nh)

    def scores(j, slot, maps=both, queries=q_ref):
        ks = pl.multiple_of(j * tile, tile)
        for hd in heads:
            kt = k_ref[0, hd, pl.ds(ks, tile), :]
            for m in maps:
                s_ref[hd, slot, m] = _dot(kt, queries[0, hd, m])

    def weighted_values(j, slot, maps=both):
        for hd in heads:
            vt = vt_ref[0, hd, j]
            for m in maps:
                acc_ref[hd, m] = alpha_ref[hd, slot, m] * acc_ref[hd, m] + _dot(vt, p_ref[hd, slot, m])

    def softmax_diagonal(slot, m):
        half = tile // 2
        for hd in heads:
            s_top = s_ref[hd, slot, m, :half, :] + bias_ref[hd, 0, :half, :]
            s_low = s_ref[hd, slot, m, half:, half:] + bias_ref[hd, 0, half:, half:]
            m_old = m_ref[hd, m]
            m_new = jnp.maximum(m_old, jnp.max(s_top, axis=0, keepdims=True))
            m_new = jnp.concatenate(
                [m_new[:, :half], jnp.maximum(m_new[:, half:], jnp.max(s_low, axis=0, keepdims=True))], axis=1)
            p_ref[hd, slot, m, :half, :] = jnp.exp2(s_top - m_new).astype(BF16)
            p_ref[hd, slot, m, half:, half:] = jnp.exp2(s_low - m_new[:, half:]).astype(BF16)
            p_ref[hd, slot, m, half:, :half] = jnp.zeros((half, half), BF16)
            alpha_ref[hd, slot, m] = jnp.exp2(m_old - m_new)
            m_ref[hd, m] = m_new

    def softmax_map(slot, near, m):
        if near == 0:
            return softmax_diagonal(slot, m)
        for hd in heads:
            s = s_ref[hd, slot, m]
            if near == 1:
                r0, c1 = tile - NEAR_KEYS, NEAR_QUERIES
                corner = s[r0:, :c1] + corner_ref[hd, 0]
                s = jnp.concatenate([s[:r0], jnp.concatenate([corner, s[r0:, c1:]], axis=1)], axis=0)
            m_old = m_ref[hd, m]
            m_new = jnp.maximum(m_old, jnp.max(s, axis=0, keepdims=True))
            p = jnp.exp2(s - m_new)
            p_ref[hd, slot, m] = p.astype(BF16)
            alpha = jnp.exp2(m_old - m_new)
            alpha_ref[hd, slot, m] = alpha
            m_ref[hd, m] = m_new

    def step(r, slot, *, near=None, first=False, last=False):
        nxt = jnp.maximum(i - r - 1, 0)
        if first:
            scores(i, slot, maps=(1,))
        if not last:
            scores(nxt, 1 - slot, maps=(0,))
        if not first:
            weighted_values(i - r + 1, 1 - slot, maps=(0,))
        softmax_map(slot, near, 0)
        if not last:
            scores(nxt, 1 - slot, maps=(1,))
        if not first:
            weighted_values(i - r + 1, 1 - slot, maps=(1,))
        if last:
            weighted_values(i - r, slot, maps=(0,))
        softmax_map(slot, near, 1)
        if last:
            weighted_values(i - r, slot, maps=(1,))

    pl.when(i == 0)(functools.partial(scores, i, 0, maps=(0,)))
    for last in (False, True):
        pl.when((i == 0) == last)(functools.partial(step, 0, 0, near=0, first=True, last=last))
    for last in (False, True):
        pl.when(jnp.logical_and(i >= 1, (i == 1) == last))(functools.partial(step, 1, 1, near=1, last=last))

    def far_pair(c, carry):
        r = 2 * c + 2
        step(r, 0)
        step(r + 1, 1)
        return carry

    n_mid = jnp.maximum(i - 2, 0)
    lax.fori_loop(0, n_mid // 2, far_pair, 0)
    pl.when(jnp.logical_and(i >= 3, i % 2 == 1))(functools.partial(step, i - 1, 0))
    pl.when(jnp.logical_and(i >= 2, i % 2 == 0))(functools.partial(step, i, 0, last=True))
    pl.when(jnp.logical_and(i >= 3, i % 2 == 1))(functools.partial(step, i, 1, last=True))

    scores(jnp.minimum(i + 1, pl.num_programs(2) - 1), 0, maps=(0,), queries=qn_ref)
    lam = _lam(lp_ref, lam0)
    for hd in heads:
        inv_l = [1.0 / acc_ref[hd, m, HD2:HD2 + 1, :] for m in both]
        ot = acc_ref[hd, 0, :HD2, :] * inv_l[0] - lam * (acc_ref[hd, 1, :HD2, :] * inv_l[1])
        scale = lax.rsqrt(jnp.mean(ot * ot, axis=0, keepdims=True) + EPS) * (1.0 - lam0)
        o_ref[0, hd] = ot * scale * dn_ref[...]


def _attn_prompt(lp, qt, kb, vtb, biases, dn_col, lam0):
    B, H, _, _, T = qt.shape
    tile = ATTN_TILE
    nh = 2 if H % 2 == 0 else 1
    bias, corner = biases
    return pl.pallas_call(
        functools.partial(_attn_prompt_kernel, tile=tile, lam0=lam0, nh=nh),
        grid=(B, H // nh, T // tile),
        in_specs=[_const_spec(lp.shape),
                  pl.BlockSpec((1, nh, 2, HD2, tile), lambda b, h, i: (b, h, 0, 0, i)),
                  pl.BlockSpec((1, nh, 2, HD2, tile), lambda b, h, i: (b, h, 0, 0, jnp.minimum(i + 1, T // tile - 1))),
                  pl.BlockSpec((1, nh, T, HD2), lambda b, h, i: (b, h, 0, 0)),
                  pl.BlockSpec((1, nh, T // tile, HD2 + ONES_ROWS, tile), lambda b, h, i: (b, h, 0, 0, 0)),
                  pl.BlockSpec((nh, 1, tile, tile), lambda b, h, i: (h, 0, 0, 0)),
                  pl.BlockSpec((nh, 1, NEAR_KEYS, NEAR_QUERIES), lambda b, h, i: (h, 0, 0, 0)),
                  _const_spec((HD2, 1))],
        out_specs=pl.BlockSpec((1, nh, HD2, tile), lambda b, h, i: (b, h, 0, i)),
        out_shape=jax.ShapeDtypeStruct((B, H, HD2, T), F32),
        scratch_shapes=[pltpu.VMEM((nh, 2, 2, tile, tile), F32),
                        pltpu.VMEM((nh, 2, 2, tile, tile), BF16),
                        pltpu.VMEM((nh, 2, 2, 1, tile), F32),
                        pltpu.VMEM((nh, 2, 1, tile), F32),
                        pltpu.VMEM((nh, 2, HD2 + ONES_ROWS, tile), F32)],
        compiler_params=_params(3),
        name="attn_prompt",
    )(lp, qt, qt, kb, vtb, bias, corner, dn_col)


def _attn_step_kernel(lp_ref, q_ref, kp_ref, vp_ref, kn_ref, vn_ref, bias_ref, dn_ref, o_ref, *, past, lam0):
    lam = _lam(lp_ref, lam0)
    dn = dn_ref[...]
    for hh in range(H_D):
        qm = jnp.concatenate(_split_maps(q_ref[0, hh]), axis=0)
        t = qm.shape[0] // 2
        kp = kp_ref[0, hh].astype(BF16)
        vp = vp_ref[0, hh].astype(BF16)
        kn = kn_ref[0, hh].astype(BF16)
        vn = vn_ref[0, hh].astype(BF16)
        bias_p = bias_ref[hh, 0, :, :past]
        bias_n = bias_ref[hh, 0, :, past:]
        s_p = _dot_nt(qm, kp) + jnp.concatenate([bias_p, bias_p], axis=0)
        s_n = _dot_nt(qm, kn) + jnp.concatenate([bias_n, bias_n], axis=0)
        mx = jnp.maximum(jnp.max(s_p, axis=-1, keepdims=True), jnp.max(s_n, axis=-1, keepdims=True))
        p_p = jnp.exp(s_p - mx)
        p_n = jnp.exp(s_n - mx)
        l = jnp.sum(p_p, axis=-1, keepdims=True) + jnp.sum(p_n, axis=-1, keepdims=True)
        acc = _dot(p_p.astype(BF16), vp) + _dot(p_n.astype(BF16), vn)
        o_ref[0, :, hh * HD2:(hh + 1) * HD2] = _attn_finish(acc[:t], l[:t], acc[t:], l[t:], lam, dn, lam0)


def _attn_step(lp, dq, k_past, v_past, dk, dv, bias, dn, lam0, layer):
    B, H, T, _ = dq.shape
    past = k_past.shape[3]
    qblk = pl.BlockSpec((1, H, T, HD2), lambda b: (b, 0, 0, 0))
    new = pl.BlockSpec((None, 1, H, T, HD2), lambda b: (layer, b, 0, 0, 0))
    old = pl.BlockSpec((None, 1, H, past, HD2), lambda b: (layer, b, 0, 0, 0))
    return pl.pallas_call(
        functools.partial(_attn_step_kernel, past=past, lam0=lam0),
        grid=(B,),
        in_specs=[_const_spec(lp.shape), qblk, old, old, new, new, _const_spec(bias.shape), _const_spec((1, HD2))],
        out_specs=pl.BlockSpec((1, T, H * HD2), lambda b: (b, 0, 0)),
        out_shape=jax.ShapeDtypeStruct((B, T, H * HD2), F32),
        compiler_params=_params(1),
        name="attn_step",
    )(lp, dq, k_past, v_past, dk, dv, bias, dn)


SUBLANES = 8
LANES = 128
HEADER = 2 * SUBLANES


def _interleave_pitch(nv):
    p = nv + SUBLANES
    return p if (p // SUBLANES) % 2 else p + SUBLANES


def _ffn_kernel(x_ref, og_ref, od_ref, wo_ref, lpost_ref, lpre_ref, lout_ref, wup_ref, cw_ref, cb_ref,
                cs_ref, wdn_ref, y_ref, cnew_ref, prev_ref, ext_ref, act_ref, tbuf_ref, pbuf_ref,
                *, n_seq, seq_rows, tiles_per_seq, d_ff, fc, od_transposed):
    ti = pl.program_id(0) % tiles_per_seq
    tm, D = x_ref.shape
    nv = seq_rows // SUBLANES
    pitch = _interleave_pitch(nv)
    n_slabs = D // LANES

    @pl.when(ti == 0)
    def _():
        prev_ref[...] = cs_ref[...]

    if od_transposed:
        od = jnp.concatenate([od_ref[0, hh].T for hh in range(H_D)], axis=1)
    else:
        od = od_ref[...]
    mix = _dot(og_ref[...].astype(BF16), wo_ref[:GV, :]) + _dot(od.astype(BF16), wo_ref[GV:, :])
    x1 = x_ref[...] + _rms(mix, lpost_ref[...])
    h = _rms(x1, lpre_ref[...])

    for l in range(n_slabs):
        for q in range(n_seq):
            for s in range(SUBLANES):
                t0 = q * seq_rows + nv * s
                tbuf_ref[l, pl.ds((q * SUBLANES + s) * pitch, nv), :] = h[t0:t0 + nv, l * LANES:(l + 1) * LANES]
    for l in range(n_slabs):
        for q in range(n_seq):
            for v in range(nv):
                r0 = q * seq_rows + SUBLANES * v
                pbuf_ref[r0:r0 + SUBLANES, l * LANES:(l + 1) * LANES] = \
                    tbuf_ref[l, pl.ds(q * SUBLANES * pitch + v, SUBLANES, stride=pitch), :]
    hp = pbuf_ref[...].astype(BF16)

    n_chunks = d_ff // fc
    split = (n_chunks + 1) // 2 + 1
    k_gelu = math.sqrt(2.0 / math.pi)
    first_row = lax.broadcasted_iota(jnp.int32, (SUBLANES, fc), 0) == 0

    def up_pair(c):
        return [_dot(hp, wup_ref[:, off:off + fc]) for off in (c * fc, d_ff + c * fc)]

    def header(block, carried):
        return jnp.where(first_row, carried, pltpu.roll(block, 1, 0))

    nxt = up_pair(0)
    for c in range(n_chunks):
        cur = nxt
        if c + 1 < n_chunks:
            nxt = up_pair(c + 1)
        halves = []
        for hf, off in enumerate((c * fc, d_ff + c * fc)):
            cols = slice(off, off + fc)
            buf = ext_ref.at[c % 2, hf]
            cw = cw_ref[:, cols]
            cb = cb_ref[:, cols]
            parts = []
            for q in range(n_seq):
                up = cur[hf][q * seq_rows:(q + 1) * seq_rows, :]
                base = q * (seq_rows + HEADER)
                buf[base:base + SUBLANES, :] = header(up[seq_rows - 2 * SUBLANES:seq_rows - SUBLANES, :],
                                                      prev_ref[q, 0:1, cols])
                buf[base + SUBLANES:base + HEADER, :] = header(up[seq_rows - SUBLANES:, :], prev_ref[q, 1:2, cols])
                buf[base + HEADER:base + HEADER + seq_rows, :] = up
                d2 = buf[base:base + seq_rows, :]
                d1 = buf[base + SUBLANES:base + SUBLANES + seq_rows, :]
                parts.append(cb + cw[0:1] * d2 + cw[1:2] * d1 + cw[2:3] * up)
                prev_ref[q, 0:1, cols] = up[seq_rows - SUBLANES - 1:seq_rows - SUBLANES, :]
                prev_ref[q, 1:2, cols] = up[seq_rows - 1:seq_rows, :]
            halves.append(parts[0] if n_seq == 1 else jnp.concatenate(parts, axis=0))
        g, u = halves
        gelu = g * (0.5 + 0.5 * jnp.tanh(g * (k_gelu + (k_gelu * 0.044715) * (g * g))))
        act_ref[:, c * fc:(c + 1) * fc] = (gelu * u).astype(BF16)
        if c + 1 == split:
            down = _dot(act_ref[:, :split * fc], wdn_ref[:split * fc, :])
    down = down + _dot(act_ref[:, split * fc:], wdn_ref[split * fc:, :])

    for l in range(n_slabs):
        tbuf_ref[l, pl.ds(0, tm), :] = down[:, l * LANES:(l + 1) * LANES]
    for l in range(n_slabs):
        for q in range(n_seq):
            for s in range(SUBLANES):
                for j in range(nv // SUBLANES):
                    t0 = q * seq_rows + nv * s + SUBLANES * j
                    pbuf_ref[t0:t0 + SUBLANES, l * LANES:(l + 1) * LANES] = \
                        tbuf_ref[l, pl.ds(q * seq_rows + SUBLANES * SUBLANES * j + s, SUBLANES, stride=SUBLANES), :]
    y_ref[...] = x1 + _rms(pbuf_ref[...], lout_ref[...])
    cnew_ref[...] = prev_ref[...]


def _ffn(x2d, og2d, od, B, T, w_out, lpost, lpre, lout, w_up, conv_w, conv_b, conv_state, w_down, layer):
    n_rows, D = x2d.shape
    f2 = w_up.shape[-1]
    d_ff = f2 // 2
    if T >= ROW_TILE:
        tm, n_seq, seq_rows, tps = ROW_TILE, 1, ROW_TILE, T // ROW_TILE
    else:
        tm, n_seq, seq_rows, tps = n_rows, B, T, 1
    row = lambda w: pl.BlockSpec((tm, w), lambda i: (i, 0))
    cs = pl.BlockSpec((n_seq, CONV_W - 1, f2), lambda i: (i // tps, 0, 0))
    od_transposed = od.ndim == 4
    if od_transposed:
        assert n_seq == 1
        od_spec = pl.BlockSpec((1, H_D, HD2, tm), lambda i: (i // tps, 0, 0, i % tps))
    else:
        od_spec = row(DA)
    fc = 256
    return pl.pallas_call(
        functools.partial(_ffn_kernel, n_seq=n_seq, seq_rows=seq_rows, tiles_per_seq=tps, d_ff=d_ff, fc=fc,
                          od_transposed=od_transposed),
        grid=(n_rows // tm,),
        in_specs=[row(D), row(GV), od_spec, _layer_weight_spec(w_out, layer), _const_spec((1, D)),
                  _const_spec((1, D)), _const_spec((1, D)), _layer_weight_spec(w_up, layer),
                  _const_spec(conv_w.shape), _const_spec((1, f2)), cs, _layer_weight_spec(w_down, layer)],
        out_specs=[row(D), cs],
        out_shape=[jax.ShapeDtypeStruct((n_rows, D), F32),
                   jax.ShapeDtypeStruct((B, CONV_W - 1, f2), F32)],
        scratch_shapes=[pltpu.VMEM((n_seq, CONV_W - 1, f2), F32),
                        pltpu.VMEM((2, 2, n_seq * (seq_rows + HEADER), fc), F32),
                        pltpu.VMEM((tm, d_ff), BF16),
                        pltpu.VMEM((D // LANES, max(tm, n_seq * SUBLANES * _interleave_pitch(seq_rows // SUBLANES)),
                                    LANES), F32),
                        pltpu.VMEM((tm, D), F32)],
        compiler_params=_params(1),
        name="ffn",
    )(x2d, og2d, od, w_out, lpost, lpre, lout, w_up, conv_w, conv_b, conv_state, w_down)


def _lambda_init(layer):
    return 0.8 - 0.6 * math.exp(-0.3 * layer)


def _layer(x2d, B, T, layer, depth, kv_slabs, past_k, past_v, gla_state, conv_state, bias, wts, ffn_w):
    (w_main, w_gd, w_gu, b_gu, gn, lp, dn, ln_pre, ln_post, lf_pre, lf_post, conv_w, conv_b) = wts
    prompt = past_k is None
    to_cast = tuple(w.reshape(-1, w.shape[-1]) for w in ffn_w if w.dtype != BF16)
    outs = _proj(x2d, B, T, ln_pre, w_main, w_gd, w_gu, b_gu, layer=layer, depth=depth, kv_slabs=kv_slabs,
                 emit_bf16=prompt, cast=to_cast)
    if to_cast:
        ffn_w = tuple(c.reshape(w.shape) for c, w in zip(outs[len(outs) - len(to_cast):], ffn_w))
        outs = outs[:len(outs) - len(to_cast)]
    w_out, w_up, w_down = ffn_w
    gq, gk, gv, gr, la, dk, dv = outs[:7]
    og, s_new = _gla(gq, gk, gv, gr, la, gla_state, gn, B, T)
    lam0 = _lambda_init(layer)
    if prompt:
        qt, kb, vtb = outs[7:]
        od = _attn_prompt(lp, qt, kb, vtb, bias, dn.reshape(-1, 1), lam0)
    else:
        od = _attn_step(lp, outs[7], past_k, past_v, dk, dv, bias, dn, lam0, layer).reshape(B * T, DA)
    y, c_new = _ffn(x2d, og.reshape(B * T, GV), od, B, T, w_out, ln_post, lf_pre, lf_post,
                    w_up, conv_w, conv_b, conv_state, w_down, layer)
    return y, (dk, dv), s_new, c_new, ffn_w


def kernel(x_prompt, x_sample, cache_k, cache_v, state_gla, state_conv, t5_table, w_in, w_gate_up, b_gate_up,
           gla_norm, lam_params, diff_norm, w_out, ln_mix_pre, ln_mix_post, ln_ffn_pre, ln_ffn_post,
           w_ffn_up, conv_w, conv_b, w_ffn_down):
    depth = w_in.shape[0]
    B, T, D = x_prompt.shape
    Bs, Ts, _ = x_sample.shape
    past = cache_k.shape[3]
    f2 = w_ffn_up.shape[2]
    gd0 = 2 * GQ + 2 * GV

    w_main, w_gd = _regroup_w_in(w_in, gd0)
    w_gu = w_gate_up.astype(BF16)
    ffn_w = (w_out, w_ffn_up, w_ffn_down)

    def layer_weights(l):
        row = lambda a: a[l].reshape(1, -1)
        return (w_main, w_gd, w_gu, row(b_gate_up), row(gla_norm), lam_params[l], row(diff_norm),
                row(ln_mix_pre), row(ln_mix_post), row(ln_ffn_pre), row(ln_ffn_post), conv_w[l], row(conv_b))

    wts = [layer_weights(l) for l in range(depth)]
    bias_prompt = (
        _bias_tiles(t5_table, [(0, 0)], ATTN_TILE, ATTN_TILE, keys_on_rows=True, exp2_shifted=True),
        _bias_tiles(t5_table, [(ATTN_TILE, ATTN_TILE - NEAR_KEYS)], NEAR_QUERIES, NEAR_KEYS,
                    keys_on_rows=True, exp2_shifted=True))
    bias_step = _bias_tiles(t5_table, [(past, 0)], Ts, past + Ts)

    y = x_prompt.reshape(B * T, D)
    kv_p, gp, cp = None, [], []
    zero_s = jnp.zeros((B, H_G, DK_G, DV_G), F32)
    zero_c = jnp.zeros((B, CONV_W - 1, f2), F32)
    for l in range(depth):
        y, kv_p, s_new, c_new, ffn_w = _layer(y, B, T, l, depth, kv_p, None, None, zero_s, zero_c, bias_prompt,
                                              wts[l], ffn_w)
        gp.append(s_new); cp.append(c_new)
    y_prompt = y.reshape(B, T, D)

    y = x_sample.reshape(Bs * Ts, D)
    kv_s, gs, cs = None, [], []
    for l in range(depth):
        y, kv_s, s_new, c_new, ffn_w = _layer(y, Bs, Ts, l, depth, kv_s, cache_k, cache_v, state_gla[l],
                                              state_conv[l], bias_step, wts[l], ffn_w)
        gs.append(s_new); cs.append(c_new)
    y_sample = y.reshape(Bs, Ts, D)

    return (y_prompt, y_sample, kv_p[0], kv_p[1], jnp.stack(gp), jnp.stack(cp),
            kv_s[0], kv_s[1], jnp.stack(gs), jnp.stack(cs))
```

```python
import functools
import math

import jax
import jax.numpy as jnp
from jax import lax
from jax.experimental import pallas as pl
from jax.experimental.pallas import tpu as pltpu

F32 = jnp.float32
BF16 = jnp.bfloat16

CHUNK = 64
H_G = 4
DK_G = 64
DV_G = 128
GATE_RANK = 16
GATE_NORMALIZER = 16.0
H_D = 4
DH_D = 64
NUM_BUCKETS = 32
MAX_DISTANCE = 128
CONV_W = 3
EPS = 1e-6

GQ = H_G * DK_G
GV = H_G * DV_G
DA = H_D * 2 * DH_D
HD2 = 2 * DH_D

VMEM_LIMIT_BYTES = 56 * 1024 * 1024
ROW_TILE = 512
GLA_ROWS = 256
ATTN_TILE = 512
ONES_ROWS = 16
T5_FAR_N = math.isqrt(2 ** 13 - 1) + 1
NEAR_KEYS = -(-(T5_FAR_N - 1) // 8) * 8
NEAR_QUERIES = -(-(T5_FAR_N - 1) // 128) * 128
NEG_INIT = -1e30
LOG2E = math.log2(math.e)


def _rms(x, w):
    return x * lax.rsqrt(jnp.mean(x * x, axis=-1, keepdims=True) + EPS) * w


def _dot(a, b):
    return jnp.dot(a, b, preferred_element_type=F32)


def _dot_nt(a, b):
    return lax.dot_general(a, b, (((1,), (1,)), ((), ())), preferred_element_type=F32)


def _params(n_grid):
    return pltpu.CompilerParams(dimension_semantics=("arbitrary",) * n_grid,
                                vmem_limit_bytes=VMEM_LIMIT_BYTES)


def _const_spec(shape):
    nd = len(shape)
    return pl.BlockSpec(shape, lambda *_: (0,) * nd, pipeline_mode=pl.Buffered(1))


def _layer_weight_spec(w, layer):
    return pl.BlockSpec((None,) + w.shape[1:], lambda *_: (layer, 0, 0), pipeline_mode=pl.Buffered(1))


def _bias_kernel(tab_ref, o_ref, *, starts, nq, nk, keys_on_rows, exp2_shifted):
    h = pl.program_id(0)
    nb = NUM_BUCKETS // 2
    max_exact = nb // 2
    shape = (nk, nq) if keys_on_rows else (nq, nk)
    q_axis, k_axis = (1, 0) if keys_on_rows else (0, 1)
    for d, (q0, k0) in enumerate(starts):
        qpos = q0 + lax.broadcasted_iota(jnp.int32, shape, q_axis)
        kpos = k0 + lax.broadcasted_iota(jnp.int32, shape, k_axis)
        rel = kpos - qpos
        ret = jnp.where(rel > 0, nb, 0)
        n = jnp.abs(rel)
        assert (nb, max_exact, MAX_DISTANCE) == (16, 8, 128)
        nn = jnp.maximum(n * n, 1)
        large = jnp.minimum(33 - lax.clz(nn), nb - 1)
        bucket = ret + jnp.where(n < max_exact, n, large)
        bias = jnp.zeros(shape, F32)
        for b in range(NUM_BUCKETS):
            bias = jnp.where(bucket == b, tab_ref[b, h], bias)
        if exp2_shifted:
            bias = (bias - tab_ref[nb - 1, h]) * LOG2E
        visible = (kpos // CHUNK) <= (qpos // CHUNK)
        o_ref[0, d] = jnp.where(visible, bias, -jnp.inf)


def _bias_tiles(t5_table, starts, nq, nk, *, keys_on_rows=False, exp2_shifted=False):
    nd = len(starts)
    shape = (nk, nq) if keys_on_rows else (nq, nk)
    return pl.pallas_call(
        functools.partial(_bias_kernel, starts=tuple(starts), nq=nq, nk=nk, keys_on_rows=keys_on_rows,
                          exp2_shifted=exp2_shifted),
        grid=(H_D,),
        in_specs=[pl.BlockSpec(memory_space=pltpu.SMEM)],
        out_specs=pl.BlockSpec((1, nd) + shape, lambda h: (h, 0, 0, 0)),
        out_shape=jax.ShapeDtypeStruct((H_D, nd) + shape, F32),
        compiler_params=_params(1),
        name="t5_bias_tiles",
    )(t5_table)


def _proj_kernel(*refs, n_seq, seq_rows, n_alias, n_cast, prompt, layer, depth):
    x_ref, ln_ref, w_ref, wgd_ref, wgu_ref, bgu_ref = refs[:6]
    for src, dst in zip(refs[6 + n_alias:6 + n_alias + n_cast], refs[len(refs) - n_cast:]):
        dst[...] = src[...].astype(BF16)
    refs = refs[:6 + n_alias] + refs[6 + n_alias + n_cast:len(refs) - n_cast]
    gq_ref, gk_ref, gv_ref, gr_ref, la_ref, dk_ref, dv_ref = refs[6 + n_alias:13 + n_alias]
    if n_alias == 0:
        for other in range(depth):
            if other != layer:
                dk_ref[other] = jnp.zeros(dk_ref.shape[1:], F32)
                dv_ref[other] = jnp.zeros(dv_ref.shape[1:], F32)
        dk_ref, dv_ref = dk_ref.at[layer], dv_ref.at[layer]
    if prompt:
        qt_ref, kb_ref, vt_ref = refs[13 + n_alias:]
    else:
        dq_ref, = refs[13 + n_alias:]
    h = _rms(x_ref[...], ln_ref[...]).astype(BF16)

    def seg(a, b):
        return _dot(h, w_ref[:, a:b])

    base = 2 * GQ + 2 * GV
    q = seg(base, base + DA) * DH_D ** -0.5
    k = seg(base + DA, base + 2 * DA)
    v = seg(base + 2 * DA, base + 3 * DA)
    gd = _dot(h, wgd_ref[...]).astype(BF16)
    z = _dot(gd, wgu_ref[...]) + bgu_ref[...]
    gq_ref[...] = seg(0, GQ) * DK_G ** -0.5
    gk_ref[...] = seg(GQ, 2 * GQ)
    gv_ref[...] = seg(2 * GQ, 2 * GQ + GV)
    gr_ref[...] = seg(2 * GQ + GV, 2 * GQ + 2 * GV)
    log_sig = jnp.minimum(z, 0.0) - jnp.log1p(jnp.exp(-jnp.abs(z)))
    la_ref[...] = log_sig / GATE_NORMALIZER
    feat = lax.broadcasted_iota(jnp.int32, (HD2, seq_rows), 0)
    for s in range(n_seq):
        rows = slice(s * seq_rows, (s + 1) * seq_rows)
        for hh in range(H_D):
            cols = slice(hh * HD2, (hh + 1) * HD2)
            dk_ref[s, hh] = k[rows, cols]
            dv_ref[s, hh] = v[rows, cols]
            if not prompt:
                dq_ref[s, hh] = q[rows, cols]
                continue
            qt = (q[rows, cols] * LOG2E).T
            qt_ref[s, hh, 0] = jnp.where(feat < DH_D, qt, 0.0).astype(BF16)
            qt_ref[s, hh, 1] = jnp.where(feat >= DH_D, qt, 0.0).astype(BF16)
            kb_ref[s, hh] = k[rows, cols].astype(BF16)
            for kb in range(seq_rows // ATTN_TILE):
                r0 = s * seq_rows + kb * ATTN_TILE
                vt_ref[s, hh, kb, :HD2] = v[r0:r0 + ATTN_TILE, cols].T.astype(BF16)
                vt_ref[s, hh, kb, HD2:] = jnp.ones((ONES_ROWS, ATTN_TILE), BF16)


def _proj(x2d, B, T, ln, w_main, w_gd, w_gu, b_gu, *, layer, depth, kv_slabs, emit_bf16, cast=()):
    n_rows, D = x2d.shape
    if T >= ROW_TILE:
        tm, n_seq, seq_rows, tps = ROW_TILE, 1, ROW_TILE, T // ROW_TILE
    else:
        tm, n_seq, seq_rows, tps = n_rows, B, T, 1
    grid = (n_rows // tm,)
    row = lambda w: pl.BlockSpec((tm, w), lambda i: (i, 0))
    head = pl.BlockSpec((n_seq, H_D, seq_rows, HD2), lambda i: (i // tps, 0, i % tps, 0))
    if kv_slabs is None:
        slab = pl.BlockSpec((depth, n_seq, H_D, seq_rows, HD2), lambda i: (0, i // tps, 0, i % tps, 0))
    else:
        slab = pl.BlockSpec((None, n_seq, H_D, seq_rows, HD2), lambda i: (layer, i // tps, 0, i % tps, 0))
    f32_rows = lambda w: jax.ShapeDtypeStruct((n_rows, w), F32)
    head_shape = lambda dt: jax.ShapeDtypeStruct((B, H_D, T, HD2), dt)
    slab_shape = jax.ShapeDtypeStruct((depth, B, H_D, T, HD2), F32)
    out_specs = [row(GQ), row(GQ), row(GV), row(GV), row(GQ), slab, slab]
    out_shape = [f32_rows(GQ), f32_rows(GQ), f32_rows(GV), f32_rows(GV), f32_rows(GQ), slab_shape, slab_shape]
    if emit_bf16:
        kt = seq_rows // ATTN_TILE
        out_specs += [pl.BlockSpec((n_seq, H_D, 2, HD2, seq_rows), lambda i: (i // tps, 0, 0, 0, i % tps)),
                      head,
                      pl.BlockSpec((n_seq, H_D, kt, HD2 + ONES_ROWS, ATTN_TILE),
                                   lambda i: (i // tps, 0, i % tps, 0, 0))]
        out_shape += [jax.ShapeDtypeStruct((B, H_D, 2, HD2, T), BF16), head_shape(BF16),
                      jax.ShapeDtypeStruct((B, H_D, T // ATTN_TILE, HD2 + ONES_ROWS, ATTN_TILE), BF16)]
    else:
        out_specs += [head]
        out_shape += [head_shape(F32)]
    args = [x2d, ln, w_main, w_gd, w_gu, b_gu]
    in_specs = [row(D), _const_spec((1, D)), _layer_weight_spec(w_main, layer), _layer_weight_spec(w_gd, layer),
                _layer_weight_spec(w_gu, layer), _const_spec((1, GQ))]
    aliases = {}
    if kv_slabs is not None:
        aliases = {len(args): 5, len(args) + 1: 6}
        args += list(kv_slabs)
        in_specs += [pl.BlockSpec(memory_space=pl.ANY)] * 2
    for w in cast:
        rows, cols = w.shape
        assert rows % (16 * grid[0]) == 0, (rows, grid)
        blk = pl.BlockSpec((rows // grid[0], cols), lambda i: (i, 0))
        args.append(w)
        in_specs.append(blk)
        out_specs.append(blk)
        out_shape.append(jax.ShapeDtypeStruct(w.shape, BF16))
    return pl.pallas_call(
        functools.partial(_proj_kernel, n_seq=n_seq, seq_rows=seq_rows, n_alias=len(aliases), n_cast=len(cast),
                          prompt=emit_bf16, layer=layer, depth=depth),
        grid=grid,
        in_specs=in_specs,
        out_specs=out_specs,
        out_shape=out_shape,
        input_output_aliases=aliases,
        compiler_params=_params(1),
        name="proj",
    )(*args)


def _gla_kernel(q_ref, k_ref, v_ref, r_ref, la_ref, s0_ref, gn_ref, o_ref, sout_ref, s_ref, *, n_chunks, nb):
    t = pl.program_id(1)
    seqs = range(nb)

    @pl.when(t == 0)
    def _():
        for b in seqs:
            for hh in range(H_G):
                s_ref[b, hh * DK_G:(hh + 1) * DK_G, :] = s0_ref[b, hh]

    R = n_chunks * CHUNK
    ri = lax.broadcasted_iota(jnp.int32, (R, R), 0)
    ci = lax.broadcasted_iota(jnp.int32, (R, R), 1)
    causal = jnp.logical_and((ri // CHUNK) == (ci // CHUNK), ci <= ri)
    head_of_lane = lax.broadcasted_iota(jnp.int32, (R, GQ), 1) // DK_G
    tril = causal.astype(BF16)
    chunk_rows = [slice(c * CHUNK, (c + 1) * CHUNK) for c in range(n_chunks)]
    vcols = [slice(hh * DV_G, (hh + 1) * DV_G) for hh in range(H_G)]
    pad = [jnp.zeros((8 - n_chunks % 8, GQ), F32)] if n_chunks % 8 else []

    la = [la_ref[b] for b in seqs]
    a1 = [x.astype(BF16) for x in la]
    r1 = [x - y.astype(F32) for x, y in zip(la, a1)]
    a2 = [x.astype(BF16) for x in r1]
    a3 = [(x - y.astype(F32)).astype(BF16) for x, y in zip(r1, a2)]
    g = [_dot(tril, a1[b]) + _dot(tril, a2[b]) + _dot(tril, a3[b]) for b in seqs]
    last_rows = [[g[b][(c + 1) * CHUNK - 1:(c + 1) * CHUNK, :] for c in range(n_chunks)] for b in seqs]
    g_last = [jnp.concatenate([jnp.broadcast_to(x, (CHUNK, GQ)) for x in last_rows[b]], axis=0) for b in seqs]
    decay = [jnp.exp(jnp.concatenate(last_rows[b] + pad, axis=0).T) for b in seqs]
    v = [v_ref[b].astype(BF16) for b in seqs]
    q_dec = [q_ref[b] * jnp.exp(g[b]) for b in seqs]
    k_inv = [(k_ref[b] * jnp.exp(-g[b])).astype(BF16) for b in seqs]
    k_end = [k_ref[b] * jnp.exp(g_last[b] - g[b]) for b in seqs]
    qm = [[jnp.where(head_of_lane == hh, q_dec[b], 0.0).astype(BF16) for hh in range(H_G)] for b in seqs]

    o_inter = [[[] for _ in range(H_G)] for _ in seqs]
    state = [s_ref[b] for b in seqs]
    for c, rows in enumerate(chunk_rows):
        for b in seqs:
            s_b = state[b].astype(BF16)
            kt = k_end[b][rows].T.astype(BF16)
            upd = []
            for hh in range(H_G):
                o_inter[b][hh].append(_dot(qm[b][hh][rows], s_b))
                upd.append(_dot(kt[hh * DK_G:(hh + 1) * DK_G, :], v[b][rows, vcols[hh]]))
            state[b] = decay[b][:, c:c + 1] * state[b] + jnp.concatenate(upd, axis=0)
    for b in seqs:
        s_ref[b] = state[b]

    gn = gn_ref[...]
    for hh in range(H_G):
        for b in seqs:
            a = jnp.where(causal, _dot_nt(qm[b][hh], k_inv[b]), 0.0)
            o = _dot(a.astype(BF16), v[b][:, vcols[hh]]) + jnp.concatenate(o_inter[b][hh], axis=0)
            r = r_ref[b, :, vcols[hh]]
            o_ref[b, :, vcols[hh]] = _rms(o, gn) * (r * (1.0 / (1.0 + jnp.exp(-r))))

    for b in seqs:
        for hh in range(H_G):
            sout_ref[b, hh] = state[b][hh * DK_G:(hh + 1) * DK_G, :]


def _gla(gq, gk, gv, gr, la, s0, gn, B, T):
    tb = min(GLA_ROWS, T)
    nb = 4 if B % 4 == 0 else 1
    r3 = lambda a: a.reshape(B, T, a.shape[-1])
    blk = lambda w: pl.BlockSpec((nb, tb, w), lambda b, t: (b, t, 0))
    st = pl.BlockSpec((nb, H_G, DK_G, DV_G), lambda b, t: (b, 0, 0, 0))
    return pl.pallas_call(
        functools.partial(_gla_kernel, n_chunks=tb // CHUNK, nb=nb),
        grid=(B // nb, T // tb),
        in_specs=[blk(GQ), blk(GQ), blk(GV), blk(GV), blk(GQ), st, _const_spec((1, DV_G))],
        out_specs=[blk(GV), st],
        out_shape=[jax.ShapeDtypeStruct((B, T, GV), F32),
                   jax.ShapeDtypeStruct((B, H_G, DK_G, DV_G), F32)],
        scratch_shapes=[pltpu.VMEM((nb, GQ, DV_G), F32)],
        compiler_params=_params(2),
        name="gla",
    )(r3(gq), r3(gk), r3(gv), r3(gr), r3(la), s0, gn)


def _lam(lp_ref, lam0):
    lp = lp_ref[...]
    t1 = jnp.sum(lp[0:1] * lp[1:2], axis=-1, keepdims=True)
    t2 = jnp.sum(lp[2:3] * lp[3:4], axis=-1, keepdims=True)
    return jnp.exp(t1) - jnp.exp(t2) + lam0


def _split_maps(q):
    lane = lax.broadcasted_iota(jnp.int32, q.shape, 1)
    return (jnp.where(lane < DH_D, q, 0.0).astype(BF16),
            jnp.where(lane >= DH_D, q, 0.0).astype(BF16))


def _attn_finish(acc0, l0, acc1, l1, lam, dn, lam0):
    o = acc0 * (1.0 / l0) - lam * (acc1 * (1.0 / l1))
    return _rms(o, dn) * (1.0 - lam0)


def _attn_prompt_kernel(lp_ref, q_ref, qn_ref, k_ref, vt_ref, bias_ref, corner_ref, dn_ref, o_ref,
                        s_ref, p_ref, alpha_ref, m_ref, acc_ref, *, tile, lam0, nh):
    i = pl.program_id(2)
    m_ref[...] = jnp.full(m_ref.shape, NEG_INIT, F32)
    both = (0, 1)
    heads = range(nh)

    def scores(j, slot, maps=both, queries=q_ref):
        ks = pl.multiple_of(j * tile, tile)
        for hd in heads:
            kt = k_ref[0, hd, pl.ds(ks, tile), :]
            for m in maps:
                s_ref[hd, slot, m] = _dot(kt, queries[0, hd, m])

    def weighted_values(j, slot, maps=both, start=False):
        for hd in heads:
            vt = vt_ref[0, hd, j]
            for m in maps:
                pv = _dot(vt, p_ref[hd, slot, m])
                acc_ref[hd, m] = pv if start else alpha_ref[hd, slot, m] * acc_ref[hd, m] + pv

    def softmax_diagonal(slot, m):
        half = tile // 2
        for hd in heads:
            s_top = s_ref[hd, slot, m, :half, :] + bias_ref[hd, 0, :half, :]
            s_low = s_ref[hd, slot, m, half:, half:] + bias_ref[hd, 0, half:, half:]
            m_old = m_ref[hd, m]
            m_new = jnp.maximum(m_old, jnp.max(s_top, axis=0, keepdims=True))
            m_new = jnp.concatenate(
                [m_new[:, :half], jnp.maximum(m_new[:, half:], jnp.max(s_low, axis=0, keepdims=True))], axis=1)
            p_ref[hd, slot, m, :half, :] = jnp.exp2(s_top - m_new).astype(BF16)
            p_ref[hd, slot, m, half:, half:] = jnp.exp2(s_low - m_new[:, half:]).astype(BF16)
            p_ref[hd, slot, m, half:, :half] = jnp.zeros((half, half), BF16)
            alpha_ref[hd, slot, m] = jnp.exp2(m_old - m_new)
            m_ref[hd, m] = m_new

    def softmax_map(slot, near, m):
        if near == 0:
            return softmax_diagonal(slot, m)
        for hd in heads:
            s = s_ref[hd, slot, m]
            if near == 1:
                r0, c1 = tile - NEAR_KEYS, NEAR_QUERIES
                corner = s[r0:, :c1] + corner_ref[hd, 0]
                s = jnp.concatenate([s[:r0], jnp.concatenate([corner, s[r0:, c1:]], axis=1)], axis=0)
            m_old = m_ref[hd, m]
            m_new = jnp.maximum(m_old, jnp.max(s, axis=0, keepdims=True))
            p = jnp.exp2(s - m_new)
            p_ref[hd, slot, m] = p.astype(BF16)
            alpha = jnp.exp2(m_old - m_new)
            alpha_ref[hd, slot, m] = alpha
            m_ref[hd, m] = m_new

    def step(r, slot, *, near=None, first=False, last=False):
        nxt = jnp.maximum(i - r - 1, 0)
        if first:
            scores(i, slot, maps=(1,))
        if not last:
            scores(nxt, 1 - slot, maps=(0,))
        if not first:
            weighted_values(i - r + 1, 1 - slot, maps=(0,), start=near == 1)
        softmax_map(slot, near, 0)
        if not last:
            scores(nxt, 1 - slot, maps=(1,))
        if not first:
            weighted_values(i - r + 1, 1 - slot, maps=(1,), start=near == 1)
        if last:
            weighted_values(i - r, slot, maps=(0,), start=first)
        softmax_map(slot, near, 1)
        if last:
            weighted_values(i - r, slot, maps=(1,), start=first)

    pl.when(i == 0)(functools.partial(scores, i, 0, maps=(0,)))
    for last in (False, True):
        pl.when((i == 0) == last)(functools.partial(step, 0, 0, near=0, first=True, last=last))
    for last in (False, True):
        pl.when(jnp.logical_and(i >= 1, (i == 1) == last))(functools.partial(step, 1, 1, near=1, last=last))

    def far_pair(c, carry):
        r = 2 * c + 2
        step(r, 0)
        step(r + 1, 1)
        return carry

    n_mid = jnp.maximum(i - 2, 0)
    lax.fori_loop(0, n_mid // 2, far_pair, 0)
    pl.when(jnp.logical_and(i >= 3, i % 2 == 1))(functools.partial(step, i - 1, 0))
    pl.when(jnp.logical_and(i >= 2, i % 2 == 0))(functools.partial(step, i, 0, last=True))
    pl.when(jnp.logical_and(i >= 3, i % 2 == 1))(functools.partial(step, i, 1, last=True))

    scores(jnp.minimum(i + 1, pl.num_programs(2) - 1), 0, maps=(0,), queries=qn_ref)
    lam = _lam(lp_ref, lam0)
    for hd in heads:
        inv_l = [1.0 / acc_ref[hd, m, HD2:HD2 + 1, :] for m in both]
        ot = acc_ref[hd, 0, :HD2, :] * inv_l[0] - lam * (acc_ref[hd, 1, :HD2, :] * inv_l[1])
        scale = lax.rsqrt(jnp.mean(ot * ot, axis=0, keepdims=True) + EPS) * (1.0 - lam0)
        o_ref[0, hd] = ot * scale * dn_ref[...]


def _attn_prompt(lp, qt, kb, vtb, biases, dn_col, lam0):
    B, H, _, _, T = qt.shape
    tile = ATTN_TILE
    nh = 2 if H % 2 == 0 else 1
    bias, corner = biases
    return pl.pallas_call(
        functools.partial(_attn_prompt_kernel, tile=tile, lam0=lam0, nh=nh),
        grid=(B, H // nh, T // tile),
        in_specs=[_const_spec(lp.shape),
                  pl.BlockSpec((1, nh, 2, HD2, tile), lambda b, h, i: (b, h, 0, 0, i)),
                  pl.BlockSpec((1, nh, 2, HD2, tile), lambda b, h, i: (b, h, 0, 0, jnp.minimum(i + 1, T // tile - 1))),
                  pl.BlockSpec((1, nh, T, HD2), lambda b, h, i: (b, h, 0, 0)),
                  pl.BlockSpec((1, nh, T // tile, HD2 + ONES_ROWS, tile), lambda b, h, i: (b, h, 0, 0, 0)),
                  pl.BlockSpec((nh, 1, tile, tile), lambda b, h, i: (h, 0, 0, 0)),
                  pl.BlockSpec((nh, 1, NEAR_KEYS, NEAR_QUERIES), lambda b, h, i: (h, 0, 0, 0)),
                  _const_spec((HD2, 1))],
        out_specs=pl.BlockSpec((1, nh, HD2, tile), lambda b, h, i: (b, h, 0, i)),
        out_shape=jax.ShapeDtypeStruct((B, H, HD2, T), F32),
        scratch_shapes=[pltpu.VMEM((nh, 2, 2, tile, tile), F32),
                        pltpu.VMEM((nh, 2, 2, tile, tile), BF16),
                        pltpu.VMEM((nh, 2, 2, 1, tile), F32),
                        pltpu.VMEM((nh, 2, 1, tile), F32),
                        pltpu.VMEM((nh, 2, HD2 + ONES_ROWS, tile), F32)],
        compiler_params=_params(3),
        name="attn_prompt",
    )(lp, qt, qt, kb, vtb, bias, corner, dn_col)


def _attn_step_kernel(lp_ref, q_ref, kp_ref, vp_ref, kn_ref, vn_ref, bias_ref, dn_ref, o_ref, *, past, lam0):
    lam = _lam(lp_ref, lam0)
    dn = dn_ref[...]
    for hh in range(H_D):
        qm = jnp.concatenate(_split_maps(q_ref[0, hh]), axis=0)
        t = qm.shape[0] // 2
        kp = kp_ref[0, hh].astype(BF16)
        vp = vp_ref[0, hh].astype(BF16)
        kn = kn_ref[0, hh].astype(BF16)
        vn = vn_ref[0, hh].astype(BF16)
        bias_p = bias_ref[hh, 0, :, :past]
        bias_n = bias_ref[hh, 0, :, past:]
        s_p = _dot_nt(qm, kp) + jnp.concatenate([bias_p, bias_p], axis=0)
        s_n = _dot_nt(qm, kn) + jnp.concatenate([bias_n, bias_n], axis=0)
        mx = jnp.maximum(jnp.max(s_p, axis=-1, keepdims=True), jnp.max(s_n, axis=-1, keepdims=True))
        p_p = jnp.exp(s_p - mx)
        p_n = jnp.exp(s_n - mx)
        l = jnp.sum(p_p, axis=-1, keepdims=True) + jnp.sum(p_n, axis=-1, keepdims=True)
        acc = _dot(p_p.astype(BF16), vp) + _dot(p_n.astype(BF16), vn)
        o_ref[0, :, hh * HD2:(hh + 1) * HD2] = _attn_finish(acc[:t], l[:t], acc[t:], l[t:], lam, dn, lam0)


def _attn_step(lp, dq, k_past, v_past, dk, dv, bias, dn, lam0, layer):
    B, H, T, _ = dq.shape
    past = k_past.shape[3]
    qblk = pl.BlockSpec((1, H, T, HD2), lambda b: (b, 0, 0, 0))
    new = pl.BlockSpec((None, 1, H, T, HD2), lambda b: (layer, b, 0, 0, 0))
    old = pl.BlockSpec((None, 1, H, past, HD2), lambda b: (layer, b, 0, 0, 0))
    return pl.pallas_call(
        functools.partial(_attn_step_kernel, past=past, lam0=lam0),
        grid=(B,),
        in_specs=[_const_spec(lp.shape), qblk, old, old, new, new, _const_spec(bias.shape), _const_spec((1, HD2))],
        out_specs=pl.BlockSpec((1, T, H * HD2), lambda b: (b, 0, 0)),
        out_shape=jax.ShapeDtypeStruct((B, T, H * HD2), F32),
        compiler_params=_params(1),
        name="attn_step",
    )(lp, dq, k_past, v_past, dk, dv, bias, dn)


SUBLANES = 8
LANES = 128
HEADER = 2 * SUBLANES


def _interleave_pitch(nv):
    p = nv + SUBLANES
    return p if (p // SUBLANES) % 2 else p + SUBLANES


def _ffn_kernel(x_ref, og_ref, od_ref, wo_ref, lpost_ref, lpre_ref, lout_ref, wup_ref, cw_ref, cb_ref,
                cs_ref, wdn_ref, y_ref, cnew_ref, prev_ref, ext_ref, act_ref, tbuf_ref, pbuf_ref,
                *, n_seq, seq_rows, tiles_per_seq, d_ff, fc, od_transposed):
    ti = pl.program_id(0) % tiles_per_seq
    tm, D = x_ref.shape
    nv = seq_rows // SUBLANES
    pitch = _interleave_pitch(nv)
    n_slabs = D // LANES

    @pl.when(ti == 0)
    def _():
        prev_ref[...] = cs_ref[...]

    if od_transposed:
        od = jnp.concatenate([od_ref[0, hh].T for hh in range(H_D)], axis=1)
    else:
        od = od_ref[...]
    mix = _dot(og_ref[...].astype(BF16), wo_ref[:GV, :]) + _dot(od.astype(BF16), wo_ref[GV:, :])
    x1 = x_ref[...] + _rms(mix, lpost_ref[...])
    h = _rms(x1, lpre_ref[...])

    for l in range(n_slabs):
        for q in range(n_seq):
            for s in range(SUBLANES):
                t0 = q * seq_rows + nv * s
                tbuf_ref[l, pl.ds((q * SUBLANES + s) * pitch, nv), :] = h[t0:t0 + nv, l * LANES:(l + 1) * LANES]
    for l in range(n_slabs):
        for q in range(n_seq):
            for v in range(nv):
                r0 = q * seq_rows + SUBLANES * v
                pbuf_ref[r0:r0 + SUBLANES, l * LANES:(l + 1) * LANES] = \
                    tbuf_ref[l, pl.ds(q * SUBLANES * pitch + v, SUBLANES, stride=pitch), :]
    hp = pbuf_ref[...].astype(BF16)

    n_chunks = d_ff // fc
    split = (n_chunks + 1) // 2 + 1
    k_gelu = math.sqrt(2.0 / math.pi)
    first_row = lax.broadcasted_iota(jnp.int32, (SUBLANES, fc), 0) == 0

    def up_pair(c):
        return [_dot(hp, wup_ref[:, off:off + fc]) for off in (c * fc, d_ff + c * fc)]

    def header(block, carried):
        return jnp.where(first_row, carried, pltpu.roll(block, 1, 0))

    nxt = up_pair(0)
    for c in range(n_chunks):
        cur = nxt
        if c + 1 < n_chunks:
            nxt = up_pair(c + 1)
        halves = []
        for hf, off in enumerate((c * fc, d_ff + c * fc)):
            cols = slice(off, off + fc)
            buf = ext_ref.at[c % 2, hf]
            cw = cw_ref[:, cols]
            cb = cb_ref[:, cols]
            parts = []
            for q in range(n_seq):
                up = cur[hf][q * seq_rows:(q + 1) * seq_rows, :]
                base = q * (seq_rows + HEADER)
                buf[base:base + SUBLANES, :] = header(up[seq_rows - 2 * SUBLANES:seq_rows - SUBLANES, :],
                                                      prev_ref[q, 0:1, cols])
                buf[base + SUBLANES:base + HEADER, :] = header(up[seq_rows - SUBLANES:, :], prev_ref[q, 1:2, cols])
                buf[base + HEADER:base + HEADER + seq_rows, :] = up
                d2 = buf[base:base + seq_rows, :]
                d1 = buf[base + SUBLANES:base + SUBLANES + seq_rows, :]
                parts.append(cb + cw[0:1] * d2 + cw[1:2] * d1 + cw[2:3] * up)
                prev_ref[q, 0:1, cols] = up[seq_rows - SUBLANES - 1:seq_rows - SUBLANES, :]
                prev_ref[q, 1:2, cols] = up[seq_rows - 1:seq_rows, :]
            halves.append(parts[0] if n_seq == 1 else jnp.concatenate(parts, axis=0))
        g, u = halves
        gelu = g * (0.5 + 0.5 * jnp.tanh(g * (k_gelu + (k_gelu * 0.044715) * (g * g))))
        act_ref[:, c * fc:(c + 1) * fc] = (gelu * u).astype(BF16)
        if c + 1 == split:
            down = _dot(act_ref[:, :split * fc], wdn_ref[:split * fc, :])
    down = down + _dot(act_ref[:, split * fc:], wdn_ref[split * fc:, :])

    for l in range(n_slabs):
        tbuf_ref[l, pl.ds(0, tm), :] = down[:, l * LANES:(l + 1) * LANES]
    for l in range(n_slabs):
        for q in range(n_seq):
            for s in range(SUBLANES):
                for j in range(nv // SUBLANES):
                    t0 = q * seq_rows + nv * s + SUBLANES * j
                    pbuf_ref[t0:t0 + SUBLANES, l * LANES:(l + 1) * LANES] = \
                        tbuf_ref[l, pl.ds(q * seq_rows + SUBLANES * SUBLANES * j + s, SUBLANES, stride=SUBLANES), :]
    y_ref[...] = x1 + _rms(pbuf_ref[...], lout_ref[...])
    cnew_ref[...] = prev_ref[...]


def _ffn(x2d, og2d, od, B, T, w_out, lpost, lpre, lout, w_up, conv_w, conv_b, conv_state, w_down, layer):
    n_rows, D = x2d.shape
    f2 = w_up.shape[-1]
    d_ff = f2 // 2
    if T >= ROW_TILE:
        tm, n_seq, seq_rows, tps = ROW_TILE, 1, ROW_TILE, T // ROW_TILE
    else:
        tm, n_seq, seq_rows, tps = n_rows, B, T, 1
    row = lambda w: pl.BlockSpec((tm, w), lambda i: (i, 0))
    cs = pl.BlockSpec((n_seq, CONV_W - 1, f2), lambda i: (i // tps, 0, 0))
    od_transposed = od.ndim == 4
    if od_transposed:
        assert n_seq == 1
        od_spec = pl.BlockSpec((1, H_D, HD2, tm), lambda i: (i // tps, 0, 0, i % tps))
    else:
        od_spec = row(DA)
    fc = 256
    return pl.pallas_call(
        functools.partial(_ffn_kernel, n_seq=n_seq, seq_rows=seq_rows, tiles_per_seq=tps, d_ff=d_ff, fc=fc,
                          od_transposed=od_transposed),
        grid=(n_rows // tm,),
        in_specs=[row(D), row(GV), od_spec, _layer_weight_spec(w_out, layer), _const_spec((1, D)),
                  _const_spec((1, D)), _const_spec((1, D)), _layer_weight_spec(w_up, layer),
                  _const_spec(conv_w.shape), _const_spec((1, f2)), cs, _layer_weight_spec(w_down, layer)],
        out_specs=[row(D), cs],
        out_shape=[jax.ShapeDtypeStruct((n_rows, D), F32),
                   jax.ShapeDtypeStruct((B, CONV_W - 1, f2), F32)],
        scratch_shapes=[pltpu.VMEM((n_seq, CONV_W - 1, f2), F32),
                        pltpu.VMEM((2, 2, n_seq * (seq_rows + HEADER), fc), F32),
                        pltpu.VMEM((tm, d_ff), BF16),
                        pltpu.VMEM((D // LANES, max(tm, n_seq * SUBLANES * _interleave_pitch(seq_rows // SUBLANES)),
                                    LANES), F32),
                        pltpu.VMEM((tm, D), F32)],
        compiler_params=_params(1),
        name="ffn",
    )(x2d, og2d, od, w_out, lpost, lpre, lout, w_up, conv_w, conv_b, conv_state, w_down)


def _lambda_init(layer):
    return 0.8 - 0.6 * math.exp(-0.3 * layer)


def _layer(x2d, B, T, layer, depth, kv_slabs, past_k, past_v, gla_state, conv_state, bias, wts, ffn_w):
    (w_main, w_gd, w_gu, b_gu, gn, lp, dn, ln_pre, ln_post, lf_pre, lf_post, conv_w, conv_b) = wts
    prompt = past_k is None
    to_cast = tuple(w.reshape(-1, w.shape[-1]) for w in ffn_w if w.dtype != BF16)
    outs = _proj(x2d, B, T, ln_pre, w_main, w_gd, w_gu, b_gu, layer=layer, depth=depth, kv_slabs=kv_slabs,
                 emit_bf16=prompt, cast=to_cast)
    if to_cast:
        ffn_w = tuple(c.reshape(w.shape) for c, w in zip(outs[len(outs) - len(to_cast):], ffn_w))
        outs = outs[:len(outs) - len(to_cast)]
    w_out, w_up, w_down = ffn_w
    gq, gk, gv, gr, la, dk, dv = outs[:7]
    og, s_new = _gla(gq, gk, gv, gr, la, gla_state, gn, B, T)
    lam0 = _lambda_init(layer)
    if prompt:
        qt, kb, vtb = outs[7:]
        od = _attn_prompt(lp, qt, kb, vtb, bias, dn.reshape(-1, 1), lam0)
    else:
        od = _attn_step(lp, outs[7], past_k, past_v, dk, dv, bias, dn, lam0, layer).reshape(B * T, DA)
    y, c_new = _ffn(x2d, og.reshape(B * T, GV), od, B, T, w_out, ln_post, lf_pre, lf_post,
                    w_up, conv_w, conv_b, conv_state, w_down, layer)
    return y, (dk, dv), s_new, c_new, ffn_w


def kernel(x_prompt, x_sample, cache_k, cache_v, state_gla, state_conv, t5_table, w_in, w_gate_up, b_gate_up,
           gla_norm, lam_params, diff_norm, w_out, ln_mix_pre, ln_mix_post, ln_ffn_pre, ln_ffn_post,
           w_ffn_up, conv_w, conv_b, w_ffn_down):
    depth = w_in.shape[0]
    B, T, D = x_prompt.shape
    Bs, Ts, _ = x_sample.shape
    past = cache_k.shape[3]
    f2 = w_ffn_up.shape[2]
    gd0 = 2 * GQ + 2 * GV

    w_main = jnp.concatenate([w_in[:, :, :gd0], w_in[:, :, gd0 + GATE_RANK:]], axis=2).astype(BF16)
    w_gd = w_in[:, :, gd0:gd0 + GATE_RANK].astype(BF16)
    w_gu = w_gate_up.astype(BF16)
    ffn_w = (w_out, w_ffn_up, w_ffn_down)

    def layer_weights(l):
        row = lambda a: a[l].reshape(1, -1)
        return (w_main, w_gd, w_gu, row(b_gate_up), row(gla_norm), lam_params[l], row(diff_norm),
                row(ln_mix_pre), row(ln_mix_post), row(ln_ffn_pre), row(ln_ffn_post), conv_w[l], row(conv_b))

    wts = [layer_weights(l) for l in range(depth)]
    bias_prompt = (
        _bias_tiles(t5_table, [(0, 0)], ATTN_TILE, ATTN_TILE, keys_on_rows=True, exp2_shifted=True),
        _bias_tiles(t5_table, [(ATTN_TILE, ATTN_TILE - NEAR_KEYS)], NEAR_QUERIES, NEAR_KEYS,
                    keys_on_rows=True, exp2_shifted=True))
    bias_step = _bias_tiles(t5_table, [(past, 0)], Ts, past + Ts)

    y = x_prompt.reshape(B * T, D)
    kv_p, gp, cp = None, [], []
    zero_s = jnp.zeros((B, H_G, DK_G, DV_G), F32)
    zero_c = jnp.zeros((B, CONV_W - 1, f2), F32)
    for l in range(depth):
        y, kv_p, s_new, c_new, ffn_w = _layer(y, B, T, l, depth, kv_p, None, None, zero_s, zero_c, bias_prompt,
                                              wts[l], ffn_w)
        gp.append(s_new); cp.append(c_new)
    y_prompt = y.reshape(B, T, D)

    y = x_sample.reshape(Bs * Ts, D)
    kv_s, gs, cs = None, [], []
    for l in range(depth):
        y, kv_s, s_new, c_new, ffn_w = _layer(y, Bs, Ts, l, depth, kv_s, cache_k, cache_v, state_gla[l],
                                              state_conv[l], bias_step, wts[l], ffn_w)
        gs.append(s_new); cs.append(c_new)
    y_sample = y.reshape(Bs, Ts, D)

    return (y_prompt, y_sample, kv_p[0], kv_p[1], jnp.stack(gp), jnp.stack(cp),
            kv_s[0], kv_s[1], jnp.stack(gs), jnp.stack(cs))
```
